```python
import math, functools
import jax, jax.numpy as jnp
from jax import lax
import numpy as np

D_MODEL = 2048
BATCH = 4
SEQ = 2048
DEPTH = 4
DEC_BATCH = 8
DEC_SEQ = 1
PAST_LEN = 16384
PAGE_SIZE = 128

N_A = DEPTH // 2
N_B = DEPTH - N_A
N_DENSE = (DEPTH + 1) // 2
N_MOE = DEPTH // 2

H_A = D_MODEL // 256
DK_A = D_MODEL // H_A
DV_A = 2 * DK_A
RET_CHUNK = 128
ROPE_BASE = 10000.0

H_B = D_MODEL // 256
D_HEAD_B = 128
Q_BLOCK = 128

D_FF = 5632
N_EXPERTS = 8
TOP_K = 2
D_EXPERT = 2816
EPS = 1e-6

kernel_name = 'yoco_retention_diffattn_step'

F32 = jnp.float32


def _rmsnorm(x, g):
    xf = x.astype(F32)
    y = xf * lax.rsqrt(jnp.mean(xf * xf, axis=-1, keepdims=True) + EPS)
    return (y * g.astype(F32)).astype(x.dtype)


def _ada(c, w, b, n):
    mod = jax.nn.silu(c) @ w + b
    return jnp.split(mod[:, None, :], n, axis=-1)


def _rotary(x, pos):
    d = x.shape[-1]
    inv = ROPE_BASE ** (-jnp.arange(0, d, 2, dtype=F32) / d)
    ang = pos.astype(F32)[:, None] * inv[None, :]
    cos = jnp.cos(ang)[None, :, None, :]
    sin = jnp.sin(ang)[None, :, None, :]
    xf = x.astype(F32)
    x1, x2 = xf[..., : d // 2], xf[..., d // 2:]
    return jnp.concatenate([x1 * cos - x2 * sin, x2 * cos + x1 * sin], axis=-1).astype(x.dtype)


def _ret_log_gamma():
    return jnp.log(1.0 - 2.0 ** (-5.0 - jnp.arange(H_A, dtype=F32)))


def _retention_chunk(S, q, k, v):
    C = q.shape[1]
    lg = _ret_log_gamma()
    n = jnp.arange(C, dtype=F32)
    diff = n[:, None] - n[None, :]
    decay = jnp.where(diff >= 0, jnp.exp(lg[:, None, None] * jnp.maximum(diff, 0.0)), 0.0).astype(q.dtype)
    inner = jnp.einsum('bnhd,bmhd->bhnm', q, k) * decay
    o = jnp.einsum('bhnm,bmhe->bnhe', inner, v)
    q_decay = jnp.exp(lg[None, :] * (n[:, None] + 1.0)).astype(q.dtype)
    o = o + jnp.einsum('bnhd,bhde->bnhe', q, S) * q_decay[None, :, :, None]
    k_decay = jnp.exp(lg[None, :] * (C - 1.0 - n[:, None])).astype(q.dtype)
    S_new = S * jnp.exp(lg * C).astype(S.dtype)[None, :, None, None] + jnp.einsum('bmhd,bmhe->bhde', k * k_decay[None, :, :, None], v)
    return S_new, o


def _head_groupnorm(o):
    of = o.astype(F32)
    mu = jnp.mean(of, axis=-1, keepdims=True)
    var = jnp.mean(jnp.square(of - mu), axis=-1, keepdims=True)
    return ((of - mu) * lax.rsqrt(var + EPS)).astype(o.dtype)


def _retention(h, w_in, w_out, S0, pos0):
    B, T, _ = h.shape
    proj = h @ w_in
    q, k, v, g = jnp.split(proj, [H_A * DK_A, 2 * H_A * DK_A, 2 * H_A * DK_A + H_A * DV_A], axis=-1)
    pos = pos0 + jnp.arange(T)
    q = _rotary(q.reshape(B, T, H_A, DK_A), pos)
    k = _rotary(k.reshape(B, T, H_A, DK_A), pos) * (DK_A ** -0.5)
    v = v.reshape(B, T, H_A, DV_A)
    C = RET_CHUNK if T % RET_CHUNK == 0 else T
    nc = T // C
    xs = tuple(a.reshape(B, nc, C, H_A, a.shape[-1]).swapaxes(0, 1) for a in (q, k, v))
    S, o = lax.scan(lambda s, qkv: _retention_chunk(s, *qkv), S0, xs)
    o = _head_groupnorm(o.swapaxes(0, 1).reshape(B, T, H_A, DV_A))
    y = (jax.nn.silu(g) * o.reshape(B, T, H_A * DV_A)) @ w_out
    return y, S


def _diff_attn_prompt(q, k, v, lam):
    B, S = q.shape[:2]
    blk = Q_BLOCK if S % Q_BLOCK == 0 else S
    nb = S // blk
    kf = k.reshape(B, S, H_B, 2, D_HEAD_B).astype(F32)
    qb = q.reshape(B, nb, blk, H_B, 2, D_HEAD_B).swapaxes(0, 1)
    starts = jnp.arange(nb) * blk
    kpos = jnp.arange(S)
    scale = D_HEAD_B ** -0.5

    def one(args):
        qi, st = args
        s = jnp.einsum('bqhcd,bkhcd->bchqk', qi.astype(F32), kf) * scale
        mask = kpos[None, :] <= (st + jnp.arange(blk))[:, None]
        p = jax.nn.softmax(jnp.where(mask, s, -jnp.inf), axis=-1)
        pd = p[:, 0] - lam * p[:, 1]
        return jnp.einsum('bhqk,bkhe->bqhe', pd.astype(v.dtype), v)

    o = lax.map(one, (qb, starts))
    return o.swapaxes(0, 1).reshape(B, S, H_B, 2 * D_HEAD_B)


def _online_update(carry, s, v):
    m, l, acc = carry
    m_new = jnp.maximum(m, jnp.max(s, axis=-1))
    corr = jnp.exp(m - m_new)
    p = jnp.exp(s - m_new[..., None])
    acc = acc * corr[..., None] + jnp.einsum('bchtk,bkhe->bchte', p, v.astype(F32))
    return (m_new, l * corr + jnp.sum(p, axis=-1), acc)


def _diff_attn_sample(q, k, v, lam, cache_k, cache_v, page_table):
    DB, T = q.shape[:2]
    qf = q.astype(F32) * (D_HEAD_B ** -0.5)

    def page_step(carry, phys):
        kp = cache_k[phys].reshape(DB, PAGE_SIZE, H_B, 2, D_HEAD_B).astype(F32)
        s = jnp.einsum('bthcd,bkhcd->bchtk', qf, kp)
        return _online_update(carry, s, cache_v[phys]), None

    init = (jnp.full((DB, 2, H_B, T), -jnp.inf, F32), jnp.zeros((DB, 2, H_B, T), F32),
            jnp.zeros((DB, 2, H_B, T, 2 * D_HEAD_B), F32))
    carry, _ = lax.scan(page_step, init, page_table.T)
    kn = k.reshape(DB, T, H_B, 2, D_HEAD_B).astype(F32)
    s = jnp.einsum('bthcd,bkhcd->bchtk', qf, kn)
    causal = jnp.arange(T)[None, :] <= jnp.arange(T)[:, None]
    m, l, acc = _online_update(carry, jnp.where(causal, s, -jnp.inf), v)
    a = acc / l[..., None]
    o = a[:, 0] - lam * a[:, 1]
    return o.transpose(0, 2, 1, 3).astype(q.dtype)


def _diff_attention(h, w_q, w_o, lam_p, subln_g, lam_init, attend):
    B, T, _ = h.shape
    q = (h @ w_q).reshape(B, T, H_B, 2, D_HEAD_B)
    lp = lam_p.astype(F32)
    lam = jnp.exp(jnp.sum(lp[0] * lp[1])) - jnp.exp(jnp.sum(lp[2] * lp[3])) + lam_init
    o = attend(q, lam)
    o = _rmsnorm(o, subln_g) * (1.0 - lam_init)
    return o.reshape(B, T, H_B * 2 * D_HEAD_B) @ w_o


def _swiglu(h, w_gate, w_up, w_down):
    return (jax.nn.silu(h @ w_gate) * (h @ w_up)) @ w_down


def _moe(h, w_router, b_router, w_gate, w_up, w_down):
    logits = h.astype(F32) @ w_router.astype(F32) + b_router.astype(F32)
    top_v, top_i = lax.top_k(logits, TOP_K)
    wts = jax.nn.softmax(top_v, axis=-1)
    gates = jnp.sum(jax.nn.one_hot(top_i, N_EXPERTS, dtype=F32) * wts[..., None], axis=-2).astype(h.dtype)
    y = jnp.zeros_like(h)
    for e in range(N_EXPERTS):
        y = y + gates[..., e:e + 1] * _swiglu(h, w_gate[e], w_up[e], w_down[e])
    return y


def _trunk(x, c, ret_states, pos0, attend, p):
    new_states = []
    k_sh = None
    v_sh = None
    for l in range(DEPTH):
        sh1, sc1, g1, sh2, sc2, g2 = _ada(c, p['w_ada'][l], p['b_ada'][l], 6)
        h = _rmsnorm(x, p['norm_mix'][l]) * (1 + sc1) + sh1
        if l < N_A:
            y, S = _retention(h, p['w_ret_in'][l], p['w_ret_out'][l], ret_states[l], pos0)
            new_states.append(S)
        else:
            j = l - N_A
            lam_init = 0.8 - 0.6 * math.exp(-0.3 * l)
            y = _diff_attention(h, p['w_q_b'][j], p['w_o_b'][j], p['lam_b'][j], p['subln_b'][j], lam_init,
                                lambda q, lam: attend(q, k_sh, v_sh, lam))
        x = x + g1 * y
        h = _rmsnorm(x, p['norm_ffn'][l]) * (1 + sc2) + sh2
        if l % 2 == 0:
            i = l // 2
            y = _swiglu(h, p['w_ffn_gate'][i], p['w_ffn_up'][i], p['w_ffn_down'][i])
        else:
            i = l // 2
            y = _moe(h, p['w_router'][i], p['b_router'][i], p['w_exp_gate'][i], p['w_exp_up'][i], p['w_exp_down'][i])
        x = x + g2 * y
        if l == N_A - 1:
            shk, sck = _ada(c, p['w_ada_kv'], p['b_ada_kv'], 2)
            hk = _rmsnorm(x, p['norm_kv']) * (1 + sck) + shk
            kv = hk @ p['w_kv']
            B, T, _ = x.shape
            k_sh = kv[..., : H_B * 2 * D_HEAD_B].reshape(B, T, H_B, 2 * D_HEAD_B)
            v_sh = kv[..., H_B * 2 * D_HEAD_B:].reshape(B, T, H_B, 2 * D_HEAD_B)
    shf, scf = _ada(c, p['w_ada_f'], p['b_ada_f'], 2)
    y = _rmsnorm(x, p['norm_f']) * (1 + scf) + shf
    return y, jnp.stack(new_states), k_sh, v_sh


def setup_inputs(seed: int = 0) -> dict:
    key = jax.random.key(seed)
    ks = jax.random.split(key, 40)
    D = D_MODEL
    n_pages = PAST_LEN // PAGE_SIZE
    n_used = DEC_BATCH * n_pages
    n_pool = n_used + (n_used + 3) // 4
    nrm = lambda k, shape, s: jax.random.normal(k, shape, F32) * s
    gain = lambda k, shape: 1.0 + 0.02 * jax.random.normal(k, shape, F32)
    perm = jax.random.permutation(ks[5], n_pool)[:n_used]
    ret_in_w = 2 * H_A * DK_A + 2 * H_A * DV_A
    wb = H_B * 2 * D_HEAD_B
    return {
        'x_prompt': nrm(ks[0], (BATCH, SEQ, D), 1.0),
        'x_sample': nrm(ks[1], (DEC_BATCH, DEC_SEQ, D), 1.0),
        'state_ret': nrm(ks[2], (N_A, DEC_BATCH, H_A, DK_A, DV_A), 0.1),
        'cache_k': nrm(ks[3], (n_pool, PAGE_SIZE, H_B, 2 * D_HEAD_B), 1.0),
        'cache_v': nrm(ks[4], (n_pool, PAGE_SIZE, H_B, 2 * D_HEAD_B), 1.0),
        'page_table': perm.reshape(DEC_BATCH, n_pages).astype(jnp.int32),
        'c_prompt': nrm(ks[6], (BATCH, D), 1.0),
        'c_sample': nrm(ks[7], (DEC_BATCH, D), 1.0),
        'w_ada': nrm(ks[8], (DEPTH, D, 6 * D), 0.3 * D ** -0.5),
        'b_ada': nrm(ks[9], (DEPTH, 6 * D), 0.02),
        'norm_mix': gain(ks[10], (DEPTH, D)),
        'norm_ffn': gain(ks[11], (DEPTH, D)),
        'w_ret_in': nrm(ks[12], (N_A, D, ret_in_w), D ** -0.5),
        'w_ret_out': nrm(ks[13], (N_A, H_A * DV_A, D), (H_A * DV_A) ** -0.5),
        'w_ada_kv': nrm(ks[14], (D, 2 * D), 0.3 * D ** -0.5),
        'b_ada_kv': nrm(ks[15], (2 * D,), 0.02),
        'norm_kv': gain(ks[16], (D,)),
        'w_kv': nrm(ks[17], (D, 2 * wb), D ** -0.5),
        'w_q_b': nrm(ks[18], (N_B, D, wb), D ** -0.5),
        'w_o_b': nrm(ks[19], (N_B, wb, D), wb ** -0.5),
        'lam_b': nrm(ks[20], (N_B, 4, D_HEAD_B), 0.1),
        'subln_b': gain(ks[21], (N_B, 2 * D_HEAD_B)),
        'w_ffn_gate': nrm(ks[22], (N_DENSE, D, D_FF), D ** -0.5),
        'w_ffn_up': nrm(ks[23], (N_DENSE, D, D_FF), D ** -0.5),
        'w_ffn_down': nrm(ks[24], (N_DENSE, D_FF, D), D_FF ** -0.5),
        'w_router': nrm(ks[25], (N_MOE, D, N_EXPERTS), D ** -0.5),
        'b_router': nrm(ks[26], (N_MOE, N_EXPERTS), 0.01),
        'w_exp_gate': nrm(ks[27], (N_MOE, N_EXPERTS, D, D_EXPERT), D ** -0.5),
        'w_exp_up': nrm(ks[28], (N_MOE, N_EXPERTS, D, D_EXPERT), D ** -0.5),
        'w_exp_down': nrm(ks[29], (N_MOE, N_EXPERTS, D_EXPERT, D), D_EXPERT ** -0.5),
        'w_ada_f': nrm(ks[30], (D, 2 * D), 0.3 * D ** -0.5),
        'b_ada_f': nrm(ks[31], (2 * D,), 0.02),
        'norm_f': gain(ks[32], (D,)),
    }


def reference(x_prompt, x_sample, state_ret, cache_k, cache_v, page_table, c_prompt, c_sample,
              w_ada, b_ada, norm_mix, norm_ffn, w_ret_in, w_ret_out, w_ada_kv, b_ada_kv, norm_kv, w_kv,
              w_q_b, w_o_b, lam_b, subln_b, w_ffn_gate, w_ffn_up, w_ffn_down, w_router, b_router,
              w_exp_gate, w_exp_up, w_exp_down, w_ada_f, b_ada_f, norm_f):
    p = dict(w_ada=w_ada, b_ada=b_ada, norm_mix=norm_mix, norm_ffn=norm_ffn, w_ret_in=w_ret_in,
             w_ret_out=w_ret_out, w_ada_kv=w_ada_kv, b_ada_kv=b_ada_kv, norm_kv=norm_kv, w_kv=w_kv,
             w_q_b=w_q_b, w_o_b=w_o_b, lam_b=lam_b, subln_b=subln_b, w_ffn_gate=w_ffn_gate,
             w_ffn_up=w_ffn_up, w_ffn_down=w_ffn_down, w_router=w_router, b_router=b_router,
             w_exp_gate=w_exp_gate, w_exp_up=w_exp_up, w_exp_down=w_exp_down,
             w_ada_f=w_ada_f, b_ada_f=b_ada_f, norm_f=norm_f)
    s0 = jnp.zeros((N_A, x_prompt.shape[0], H_A, DK_A, DV_A), x_prompt.dtype)
    y_prompt, ret_state_prompt, k_prompt, v_prompt = _trunk(x_prompt, c_prompt, s0, 0, _diff_attn_prompt, p)
    past_len = page_table.shape[1] * PAGE_SIZE
    attend_s = functools.partial(_diff_attn_sample, cache_k=cache_k, cache_v=cache_v, page_table=page_table)
    y_sample, ret_state_sample, k_sample, v_sample = _trunk(x_sample, c_sample, state_ret, past_len, attend_s, p)
    return (y_prompt, y_sample, ret_state_prompt, ret_state_sample, k_prompt, v_prompt, k_sample, v_sample)
```

```python
import functools
import math

import numpy as np
import jax
import jax.numpy as jnp
from jax import lax
from jax.experimental import pallas as pl
from jax.experimental.pallas import tpu as pltpu

F32 = jnp.float32
BF16 = jnp.bfloat16
I32 = jnp.int32
EPS = 1e-6
RET_CHUNK = 128
ROPE_BASE = 10000.0
TOP_K = 2
NEG = -1e30
MOD_ROWS = 8
VMEM_LIMIT = 56 << 20


def _cparams(sem):
    return pltpu.CompilerParams(dimension_semantics=sem, vmem_limit_bytes=VMEM_LIMIT)


def _tile(n, pref, mult=8):
    if n <= pref:
        return n
    for t in range(pref, 0, -1):
        if n % t == 0 and t % mult == 0:
            return t
    return n


def _bf(x):
    return x.astype(BF16)


def _dot(a, b):
    return jnp.dot(a, b, preferred_element_type=F32)


def _dot_nt(a, b):
    return lax.dot_general(a, b, (((1,), (1,)), ((), ())), preferred_element_type=F32)


def _dot_tn(a, b):
    return lax.dot_general(a, b, (((0,), (0,)), ((), ())), preferred_element_type=F32)


def _norm_mod(x, gain, scale, shift):
    y = x * lax.rsqrt(jnp.mean(x * x, axis=-1, keepdims=True) + EPS)
    return (y * gain) * (1.0 + scale) + shift


class _Group:
    def __init__(self, m, seq_len, bm, per_row, row_block):
        self.m = m
        self.seq_len = seq_len
        self.bm = bm
        self.per_row = per_row
        self.row_block = row_block
        self.tps = max(seq_len // bm, 1)


def _mod_spec(grp, layer, chunk, d, grid_rank):
    idx = (layer, chunk, grp.row_block, 0)
    if grid_rank == 1:
        return pl.BlockSpec((None, None, MOD_ROWS, d), lambda i: idx)
    return pl.BlockSpec((None, None, MOD_ROWS, d), lambda i, j: idx)


def _mod_row(ref, grp, i):
    if grp.per_row:
        return ref[...]
    return ref[pl.ds(i // grp.tps, 1), :]


def _ada_kernel(c_ref, w_ref, b_ref, o_ref):
    a = _bf(jax.nn.silu(c_ref[...]))
    o_ref[...] = _dot(a, _bf(w_ref[...])) + b_ref[...]


def _ada(c_rows, w, b, nch):
    L, d, n = w.shape
    rows = c_rows.shape[0]
    bn = _tile(d, 1024, 128)
    nj = d // bn
    return pl.pallas_call(
        _ada_kernel,
        grid=(L, nch, nj),
        in_specs=[pl.BlockSpec((rows, d), lambda l, c, j: (0, 0)),
                  pl.BlockSpec((None, d, bn), lambda l, c, j: (l, 0, c * nj + j)),
                  pl.BlockSpec((None, 1, bn), lambda l, c, j: (l, 0, c * nj + j))],
        out_specs=pl.BlockSpec((None, None, rows, bn), lambda l, c, j: (l, c, 0, j)),
        out_shape=jax.ShapeDtypeStruct((L, nch, rows, d), F32),
        compiler_params=_cparams(("arbitrary", "arbitrary", "arbitrary")),
        name="ada",
    )(c_rows, w, b.reshape(L, 1, n))


def _nm_kernel(*refs, grp, segs, bn, rope):
    x_ref, g_ref, sh_ref, sc_ref, w_ref = refs[:5]
    k = 5
    if rope:
        cos_ref, sin_ref = refs[5:7]
        k = 7
    outs = refs[k:k + len(segs)]
    hs_ref = refs[k + len(segs)]
    i = pl.program_id(0)
    j = pl.program_id(1)

    @pl.when(j == 0)
    def _():
        h = _norm_mod(x_ref[...], g_ref[...], _mod_row(sc_ref, grp, i), _mod_row(sh_ref, grp, i))
        hs_ref[...] = _bf(h)

    acc = _dot(hs_ref[...], _bf(w_ref[...]))
    t0 = 0
    for (width, dtype, kind, scale), o_ref in zip(segs, outs):
        nt = width // bn

        def _store(o_ref=o_ref, dtype=dtype, kind=kind, scale=scale):
            if kind == "rope":
                half = 128
                parts = []
                for hh in range(bn // (2 * half)):
                    x1 = acc[:, hh * 2 * half:hh * 2 * half + half]
                    x2 = acc[:, hh * 2 * half + half:(hh + 1) * 2 * half]
                    c = cos_ref[...]
                    s = sin_ref[...]
                    parts.append((x1 * c - x2 * s) * scale)
                    parts.append((x2 * c + x1 * s) * scale)
                o_ref[...] = jnp.concatenate(parts, axis=-1).astype(dtype)
            else:
                o_ref[...] = acc.astype(dtype)

        pl.when((j >= t0) & (j < t0 + nt))(_store)
        t0 += nt


def _nm_matmul(grp, x, gain, layer, mod, ch_shift, ch_scale, w, wl, segs, rope_tabs=None, bn=512):
    m, d = x.shape
    n = w.shape[-1]
    bm = grp.bm
    bn = _tile(n, bn, 256)
    assert all(s[0] % bn == 0 for s in segs) and sum(s[0] for s in segs) == n
    nj = n // bn
    in_specs = [pl.BlockSpec((bm, d), lambda i, j: (i, 0)),
                pl.BlockSpec((None, 1, d), lambda i, j: (layer, 0, 0)),
                _mod_spec(grp, mod[1], ch_shift, d, 2),
                _mod_spec(grp, mod[1], ch_scale, d, 2),
                pl.BlockSpec((None, d, bn), lambda i, j: (wl, 0, j))]
    args = [x, gain.reshape(gain.shape[0], 1, d), mod[0], mod[0], w]
    if rope_tabs is not None:
        tps = grp.tps
        rows = rope_tabs[0].shape[0]
        rb = min(bm, rows)
        in_specs += [pl.BlockSpec((rb, 128), lambda i, j: (i % tps, 0))] * 2
        args += list(rope_tabs)
    out_specs, out_shape, t0 = [], [], 0
    for width, dtype, _, _ in segs:
        nt = width // bn
        out_specs.append(pl.BlockSpec(
            (bm, bn), lambda i, j, t0=t0, nt=nt: (i, jnp.clip(j - t0, 0, nt - 1))))
        out_shape.append(jax.ShapeDtypeStruct((m, width), dtype))
        t0 += nt
    return pl.pallas_call(
        functools.partial(_nm_kernel, grp=grp, segs=segs, bn=bn, rope=rope_tabs is not None),
        grid=(m // bm, nj),
        in_specs=in_specs,
        out_specs=out_specs,
        out_shape=out_shape,
        scratch_shapes=[pltpu.VMEM((bm, d), BF16)],
        compiler_params=_cparams(("arbitrary", "arbitrary")),
        name="norm_matmul",
    )(*args)


def _mr_kernel(a_ref, w_ref, r_ref, gt_ref, o_ref, *, grp):
    i = pl.program_id(0)
    acc = _dot(a_ref[...], _bf(w_ref[...]))
    o_ref[...] = r_ref[...] + _mod_row(gt_ref, grp, i) * acc


def _matmul_resid(grp, a, w, wl, resid, mod, ch_gate, bn=512):
    m, k = a.shape
    n = w.shape[-1]
    bm = grp.bm
    bn = _tile(n, bn, 128)
    return pl.pallas_call(
        functools.partial(_mr_kernel, grp=grp),
        grid=(m // bm, n // bn),
        in_specs=[pl.BlockSpec((bm, k), lambda i, j: (i, 0)),
                  pl.BlockSpec((None, k, bn), lambda i, j: (wl, 0, j)),
                  pl.BlockSpec((bm, bn), lambda i, j: (i, j)),
                  pl.BlockSpec((None, None, MOD_ROWS, bn),
                               lambda i, j: (mod[1], ch_gate, grp.row_block, j))],
        out_specs=pl.BlockSpec((bm, bn), lambda i, j: (i, j)),
        out_shape=jax.ShapeDtypeStruct((m, n), F32),
        compiler_params=_cparams(("arbitrary", "arbitrary")),
        name="matmul_resid",
    )(a, w, resid, mod[0])


def _ffn_kernel(x_ref, g_ref, sh_ref, sc_ref, gt_ref, wg_ref, wu_ref, wd_ref, o_ref, hs_ref, *, grp, nf):
    i = pl.program_id(0)
    j = pl.program_id(1)

    @pl.when(j == 0)
    def _():
        h = _norm_mod(x_ref[...], g_ref[...], _mod_row(sc_ref, grp, i), _mod_row(sh_ref, grp, i))
        hs_ref[...] = _bf(h)

    hs = hs_ref[...]
    act = _bf(jax.nn.silu(_dot(hs, _bf(wg_ref[...]))) * _dot(hs, _bf(wu_ref[...])))
    part = _dot(act, _bf(wd_ref[...]))

    @pl.when(j == 0)
    def _():
        o_ref[...] = part

    @pl.when(j > 0)
    def _():
        o_ref[...] += part

    @pl.when(j == nf - 1)
    def _():
        o_ref[...] = x_ref[...] + _mod_row(gt_ref, grp, i) * o_ref[...]


def _ffn_dense(grp, x, gain, layer, mod, wg, wu, wd, wl):
    m, d = x.shape
    f = wg.shape[-1]
    bm = grp.bm
    bf = _tile(f, 256, 128)
    nf = f // bf
    return pl.pallas_call(
        functools.partial(_ffn_kernel, grp=grp, nf=nf),
        grid=(m // bm, nf),
        in_specs=[pl.BlockSpec((bm, d), lambda i, j: (i, 0), pipeline_mode=pl.Buffered(1)),
                  pl.BlockSpec((None, 1, d), lambda i, j: (layer, 0, 0)),
                  _mod_spec(grp, mod[1], 3, d, 2),
                  _mod_spec(grp, mod[1], 4, d, 2),
                  _mod_spec(grp, mod[1], 5, d, 2),
                  pl.BlockSpec((None, d, bf), lambda i, j: (wl, 0, j)),
                  pl.BlockSpec((None, d, bf), lambda i, j: (wl, 0, j)),
                  pl.BlockSpec((None, bf, d), lambda i, j: (wl, j, 0))],
        out_specs=pl.BlockSpec((bm, d), lambda i, j: (i, 0)),
        out_shape=jax.ShapeDtypeStruct((m, d), F32),
        scratch_shapes=[pltpu.VMEM((bm, d), BF16)],
        compiler_params=_cparams(("arbitrary", "arbitrary")),
        name="ffn_dense",
    )(x, gain.reshape(gain.shape[0], 1, d), mod[0], mod[0], mod[0], wg, wu, wd)


def _router_kernel(x_ref, g_ref, sh_ref, sc_ref, wr_ref, br_ref, h_ref, idx_ref, gw_ref, *, grp):
    i = pl.program_id(0)
    h = _norm_mod(x_ref[...], g_ref[...], _mod_row(sc_ref, grp, i), _mod_row(sh_ref, grp, i))
    h_ref[...] = h
    logits = jnp.dot(h, wr_ref[...], precision=lax.Precision.HIGHEST,
                     preferred_element_type=F32) + br_ref[...]
    lane = lax.broadcasted_iota(I32, logits.shape, 1).astype(F32)
    big = float(logits.shape[-1])
    v0 = jnp.max(logits, axis=-1, keepdims=True)
    i0 = jnp.min(jnp.where(logits == v0, lane, big), axis=-1, keepdims=True)
    rest = jnp.where(lane == i0, -jnp.inf, logits)
    v1 = jnp.max(rest, axis=-1, keepdims=True)
    i1 = jnp.min(jnp.where(rest == v1, lane, big), axis=-1, keepdims=True)
    e = jnp.exp(v1 - v0)
    w0 = 1.0 / (1.0 + e)
    w1 = e / (1.0 + e)
    idx_ref[...] = jnp.where(lane == 0.0, i0, jnp.where(lane == 1.0, i1, 0.0)).astype(I32)
    gw_ref[...] = jnp.where(lane == 0.0, w0, jnp.where(lane == 1.0, w1, 0.0))


def _router(grp, x, gain, layer, mod, w_router, b_router, wl):
    m, d = x.shape
    ne = 128
    pad = ne - w_router.shape[-1]
    w_router = jnp.pad(w_router, ((0, 0), (0, 0), (0, pad)))
    b_router = jnp.pad(b_router, ((0, 0), (0, pad)), constant_values=NEG)
    bm = min(grp.bm, 512)
    g2 = _Group(grp.m, grp.seq_len, bm, grp.per_row, grp.row_block)
    return pl.pallas_call(
        functools.partial(_router_kernel, grp=g2),
        grid=(m // bm,),
        in_specs=[pl.BlockSpec((bm, d), lambda i: (i, 0)),
                  pl.BlockSpec((None, 1, d), lambda i: (layer, 0, 0)),
                  _mod_spec(g2, mod[1], 3, d, 1),
                  _mod_spec(g2, mod[1], 4, d, 1),
                  pl.BlockSpec((None, d, ne), lambda i: (wl, 0, 0)),
                  pl.BlockSpec((None, 1, ne), lambda i: (wl, 0, 0))],
        out_specs=[pl.BlockSpec((bm, d), lambda i: (i, 0)),
                   pl.BlockSpec((bm, 128), lambda i: (i, 0)),
                   pl.BlockSpec((bm, 128), lambda i: (i, 0))],
        out_shape=[jax.ShapeDtypeStruct((m, d), F32),
                   jax.ShapeDtypeStruct((m, 128), I32),
                   jax.ShapeDtypeStruct((m, 128), F32)],
        compiler_params=_cparams(("arbitrary",)),
        name="router",
    )(x, gain.reshape(gain.shape[0], 1, d), mod[0], mod[0], w_router,
      b_router.reshape(b_router.shape[0], 1, ne))


def _gather_rows(idx_ref, n, src_hbm, dst, sem):
    def issue(r, c):
        pltpu.make_async_copy(src_hbm.at[pl.ds(idx_ref[0, r], 1)], dst.at[pl.ds(r, 1)], sem).start()
        return c
    lax.fori_loop(0, n, issue, 0)


def _wait_rows(n, src_hbm, dst, sem):
    def wait(r, c):
        pltpu.make_async_copy(src_hbm.at[pl.ds(0, 1)], dst.at[pl.ds(0, 1)], sem).wait()
        return c
    lax.fori_loop(0, n, wait, 0)


def _moe_kernel(te_ref, nv_ref, tok_ref, h_hbm, wg_ref, wu_ref, wd_ref, o_ref, xg_ref, hs_ref, sem, *, bm):
    t = pl.program_id(0)
    j = pl.program_id(1)
    valid = t < nv_ref[0]

    @pl.when(valid & (j == 0))
    def _():
        _gather_rows(tok_ref, bm, h_hbm, xg_ref, sem)
        _wait_rows(bm, h_hbm, xg_ref, sem)
        hs_ref[...] = _bf(xg_ref[...])

    @pl.when(jnp.logical_not(valid) & (j == 0))
    def _():
        o_ref[...] = jnp.zeros_like(o_ref)

    @pl.when(valid)
    def _():
        hs = hs_ref[...]
        act = _bf(jax.nn.silu(_dot(hs, _bf(wg_ref[...]))) * _dot(hs, _bf(wu_ref[...])))
        part = _dot(act, _bf(wd_ref[...]))

        @pl.when(j == 0)
        def _():
            o_ref[...] = part

        @pl.when(j > 0)
        def _():
            o_ref[...] += part


def _moe_experts(h, tok_sorted, tile_expert, n_valid, wg, wu, wd, wl, bm):
    m, d = h.shape
    f = wg.shape[-1]
    nt = tile_expert.shape[0]
    bf = _tile(f, 256, 128)
    nf = f // bf

    def fj(t, j, nv):
        return jnp.where(t < nv[0], j, nf - 1)

    grid_spec = pltpu.PrefetchScalarGridSpec(
        num_scalar_prefetch=2,
        grid=(nt, nf),
        in_specs=[pl.BlockSpec((None, 1, bm), lambda t, j, te, nv: (t, 0, 0), memory_space=pltpu.SMEM),
                  pl.BlockSpec(memory_space=pl.ANY),
                  pl.BlockSpec((None, None, d, bf), lambda t, j, te, nv: (wl, te[t], 0, fj(t, j, nv))),
                  pl.BlockSpec((None, None, d, bf), lambda t, j, te, nv: (wl, te[t], 0, fj(t, j, nv))),
                  pl.BlockSpec((None, None, bf, d), lambda t, j, te, nv: (wl, te[t], fj(t, j, nv), 0))],
        out_specs=pl.BlockSpec((bm, d), lambda t, j, te, nv: (t, 0)),
        scratch_shapes=[pltpu.VMEM((bm, d), F32), pltpu.VMEM((bm, d), BF16), pltpu.SemaphoreType.DMA(())],
    )
    return pl.pallas_call(
        functools.partial(_moe_kernel, bm=bm),
        grid_spec=grid_spec,
        out_shape=jax.ShapeDtypeStruct((nt * bm, d), F32),
        compiler_params=_cparams(("arbitrary", "arbitrary")),
        name="moe_experts",
    )(tile_expert, n_valid, tok_sorted.reshape(nt, 1, bm), h, wg, wu, wd)


def _combine_kernel(p0_ref, p1_ref, ys_hbm, x_ref, gw_ref, gt_ref, o_ref, ya_ref, yb_ref, sem, *, grp, bm):
    i = pl.program_id(0)
    _gather_rows(p0_ref, bm, ys_hbm, ya_ref, sem)
    _gather_rows(p1_ref, bm, ys_hbm, yb_ref, sem)
    _wait_rows(2 * bm, ys_hbm, ya_ref, sem)
    gw = gw_ref[...]
    y = gw[:, 0:1] * ya_ref[...] + gw[:, 1:2] * yb_ref[...]
    o_ref[...] = x_ref[...] + _mod_row(gt_ref, grp, i) * y


def _moe_combine(grp, x, ys, pos, gw, mod):
    m, d = x.shape
    bm = min(grp.bm, 256)
    g2 = _Group(grp.m, grp.seq_len, bm, grp.per_row, grp.row_block)
    nt = m // bm
    p0 = pos[:, 0].reshape(nt, 1, bm)
    p1 = pos[:, 1].reshape(nt, 1, bm)
    return pl.pallas_call(
        functools.partial(_combine_kernel, grp=g2, bm=bm),
        grid=(nt,),
        in_specs=[pl.BlockSpec((None, 1, bm), lambda i: (i, 0, 0), memory_space=pltpu.SMEM),
                  pl.BlockSpec((None, 1, bm), lambda i: (i, 0, 0), memory_space=pltpu.SMEM),
                  pl.BlockSpec(memory_space=pl.ANY),
                  pl.BlockSpec((bm, d), lambda i: (i, 0)),
                  pl.BlockSpec((bm, 128), lambda i: (i, 0)),
                  _mod_spec(g2, mod[1], 5, d, 1)],
        out_specs=pl.BlockSpec((bm, d), lambda i: (i, 0)),
        out_shape=jax.ShapeDtypeStruct((m, d), F32),
        scratch_shapes=[pltpu.VMEM((bm, d), F32), pltpu.VMEM((bm, d), F32), pltpu.SemaphoreType.DMA(())],
        compiler_params=_cparams(("arbitrary",)),
        name="moe_combine",
    )(p0, p1, ys, x, gw, mod[0])


def _moe(grp, x, gain, layer, mod, w_router, b_router, wg, wu, wd, wl, bm_tile):
    m, d = x.shape
    ne = w_router.shape[-1]
    h, idx, gw = _router(grp, x, gain, layer, mod, w_router, b_router, wl)
    e_flat = idx[:, :TOP_K].reshape(-1)
    onehot = (e_flat[:, None] == jnp.arange(ne, dtype=I32)[None, :]).astype(I32)
    csum = jnp.cumsum(onehot, axis=0)
    rank = jnp.take_along_axis(csum, e_flat[:, None], axis=1)[:, 0] - 1
    counts = csum[-1]
    tiles_e = (counts + bm_tile - 1) // bm_tile
    tile_end = jnp.cumsum(tiles_e)
    pos = (tile_end - tiles_e)[e_flat] * bm_tile + rank
    n_valid = tile_end[-1:]
    nt = (TOP_K * m) // bm_tile + ne
    tile_expert = jnp.minimum(jnp.searchsorted(tile_end, jnp.arange(nt, dtype=I32), side="right"),
                              ne - 1).astype(I32)
    tile_expert = jnp.where(jnp.arange(nt) < n_valid[0], tile_expert, tile_expert[jnp.maximum(n_valid[0] - 1, 0)])
    tok_sorted = jnp.zeros((nt * bm_tile,), I32).at[pos].set(jnp.arange(TOP_K * m, dtype=I32) // TOP_K)
    ys = _moe_experts(h, tok_sorted, tile_expert, n_valid.astype(I32), wg, wu, wd, wl, bm_tile)
    return _moe_combine(grp, x, ys, pos.reshape(m, TOP_K).astype(I32), gw, mod)


def _ret_log_gamma(nh):
    return np.log(1.0 - 2.0 ** (-5.0 - np.arange(nh, dtype=np.float64)))


def _groupnorm(o):
    mu = jnp.mean(o, axis=-1, keepdims=True)
    dlt = o - mu
    var = jnp.mean(dlt * dlt, axis=-1, keepdims=True)
    return dlt * lax.rsqrt(var + EPS)


def _ret_kernel(q_ref, k_ref, v_ref, g_ref, dec_ref, qd_ref, kd_ref, o_ref, s_ref, *, nh, dk, dv, gamma_c):
    c = pl.program_id(1)

    @pl.when(c == 0)
    def _():
        s_ref[...] = jnp.zeros_like(s_ref)

    for h in range(nh):
        q = q_ref[:, h * dk:(h + 1) * dk]
        kf = k_ref[:, h * dk:(h + 1) * dk]
        v = v_ref[:, h * dv:(h + 1) * dv]
        st = s_ref[h]
        inner = _bf(_dot_nt(q, _bf(kf)) * dec_ref[h])
        o = _dot(inner, v) + _dot(q, _bf(st)) * qd_ref[h]
        kv = _dot_tn(_bf(kf * kd_ref[h]), v)
        s_ref[h] = st * gamma_c[h] + kv
        gt = g_ref[:, h * dv:(h + 1) * dv]
        o_ref[:, h * dv:(h + 1) * dv] = _bf(jax.nn.silu(gt) * _groupnorm(o))


def _retention_prompt(q, k, v, g, nb, t, nh):
    m = q.shape[0]
    dk = q.shape[1] // nh
    dv = v.shape[1] // nh
    c = RET_CHUNK if t % RET_CHUNK == 0 else t
    nc = t // c
    lg = _ret_log_gamma(nh)
    n = np.arange(c, dtype=np.float64)
    diff = n[:, None] - n[None, :]
    decay = np.where(diff >= 0, np.exp(lg[:, None, None] * np.maximum(diff, 0.0)), 0.0)
    qd = np.broadcast_to(np.exp(lg[:, None] * (n[None, :] + 1.0))[:, :, None], (nh, c, dv))
    kd = np.broadcast_to(np.exp(lg[:, None] * (c - 1.0 - n[None, :]))[:, :, None], (nh, c, dk))
    gamma_c = tuple(float(np.float32(np.exp(lg[h] * c))) for h in range(nh))
    out, state = pl.pallas_call(
        functools.partial(_ret_kernel, nh=nh, dk=dk, dv=dv, gamma_c=gamma_c),
        grid=(nb, nc),
        in_specs=[pl.BlockSpec((c, nh * dk), lambda b, i: (b * nc + i, 0)),
                  pl.BlockSpec((c, nh * dk), lambda b, i: (b * nc + i, 0)),
                  pl.BlockSpec((c, nh * dv), lambda b, i: (b * nc + i, 0)),
                  pl.BlockSpec((c, nh * dv), lambda b, i: (b * nc + i, 0)),
                  pl.BlockSpec((nh, c, c), lambda b, i: (0, 0, 0)),
                  pl.BlockSpec((nh, c, dv), lambda b, i: (0, 0, 0)),
                  pl.BlockSpec((nh, c, dk), lambda b, i: (0, 0, 0))],
        out_specs=[pl.BlockSpec((c, nh * dv), lambda b, i: (b * nc + i, 0)),
                   pl.BlockSpec((None, nh, dk, dv), lambda b, i: (b, 0, 0, 0))],
        out_shape=[jax.ShapeDtypeStruct((m, nh * dv), BF16),
                   jax.ShapeDtypeStruct((nb, nh, dk, dv), F32)],
        compiler_params=_cparams(("arbitrary", "arbitrary")),
        name="retention_chunk",
    )(q, k, v, g, jnp.asarray(decay, F32), jnp.asarray(qd, F32), jnp.asarray(kd, F32))
    return out, state


def _ret_step_kernel(q_ref, k_ref, v_ref, g_ref, s_ref, o_ref, so_ref, *, nh, dv, gamma):
    for h in range(nh):
        qc = q_ref[h]
        kc = k_ref[h]
        v = v_ref[:, h * dv:(h + 1) * dv].astype(F32)
        st = s_ref[h]
        qk = jnp.sum(qc * kc, axis=0, keepdims=True)
        qs = jnp.sum(st * qc, axis=0, keepdims=True)
        o = qk * v + qs * gamma[h]
        so_ref[h] = st * gamma[h] + kc * v
        gt = g_ref[:, h * dv:(h + 1) * dv]
        o_ref[:, h * dv:(h + 1) * dv] = _bf(jax.nn.silu(gt) * _groupnorm(o))


def _retention_step(q, k, v, g, state, nh):
    nb = q.shape[0]
    dk = q.shape[1] // nh
    dv = v.shape[1] // nh
    lg = _ret_log_gamma(nh)
    gamma = tuple(float(np.float32(np.exp(lg[h]))) for h in range(nh))
    qc = q.astype(F32).reshape(nb, nh, dk, 1)
    kc = k.reshape(nb, nh, dk, 1)
    out, s_new = pl.pallas_call(
        functools.partial(_ret_step_kernel, nh=nh, dv=dv, gamma=gamma),
        grid=(nb,),
        in_specs=[pl.BlockSpec((None, nh, dk, 1), lambda b: (b, 0, 0, 0)),
                  pl.BlockSpec((None, nh, dk, 1), lambda b: (b, 0, 0, 0)),
                  pl.BlockSpec((None, 1, nh * dv), lambda b: (b, 0, 0)),
                  pl.BlockSpec((None, 1, nh * dv), lambda b: (b, 0, 0)),
                  pl.BlockSpec((None, nh, dk, dv), lambda b: (b, 0, 0, 0))],
        out_specs=[pl.BlockSpec((None, 1, nh * dv), lambda b: (b, 0, 0)),
                   pl.BlockSpec((None, nh, dk, dv), lambda b: (b, 0, 0, 0))],
        out_shape=[jax.ShapeDtypeStruct((nb, 1, nh * dv), BF16),
                   jax.ShapeDtypeStruct((nb, nh, dk, dv), F32)],
        compiler_params=_cparams(("arbitrary",)),
        name="retention_step",
    )(qc, kc, v.reshape(nb, 1, nh * dv), g.reshape(nb, 1, nh * dv), state)
    return out.reshape(nb, nh * dv), s_new


def _lam(lam_ref, lam_init):
    lp = lam_ref[...]
    a = jnp.sum(lp[0:1] * lp[1:2], axis=-1, keepdims=True)
    b = jnp.sum(lp[2:3] * lp[3:4], axis=-1, keepdims=True)
    return jnp.exp(a) - jnp.exp(b) + lam_init


def _subln(o, gain, lam_init):
    y = o * lax.rsqrt(jnp.mean(o * o, axis=-1, keepdims=True) + EPS)
    return (y * gain) * (1.0 - lam_init)


def _attn_kernel(q_ref, k_ref, v_ref, lam_ref, sg_ref, o_ref, *, blk, dh, lam_init):
    qi = pl.program_id(2)
    scale = dh ** -0.5
    q = q_ref[...]
    q0, q1 = q[:, :dh], q[:, dh:]
    row = qi * blk + lax.broadcasted_iota(I32, (blk, blk), 0)

    def body(j, carry):
        m0, l0, a0, m1, l1, a1 = carry
        kb = _bf(k_ref[pl.ds(pl.multiple_of(j * blk, blk), blk), :])
        vb = _bf(v_ref[pl.ds(pl.multiple_of(j * blk, blk), blk), :])
        keep = (j * blk + lax.broadcasted_iota(I32, (blk, blk), 1)) <= row

        def upd(qc, kc, m, l, a):
            s = jnp.where(keep, _dot_nt(qc, kc) * scale, NEG)
            m_new = jnp.maximum(m, jnp.max(s, axis=-1, keepdims=True))
            alpha = jnp.exp(m - m_new)
            p = jnp.exp(s - m_new)
            return m_new, alpha * l + jnp.sum(p, axis=-1, keepdims=True), alpha * a + _dot(_bf(p), vb)

        m0, l0, a0 = upd(q0, kb[:, :dh], m0, l0, a0)
        m1, l1, a1 = upd(q1, kb[:, dh:], m1, l1, a1)
        return m0, l0, a0, m1, l1, a1

    init_m = jnp.full((blk, 1), NEG, F32)
    init_l = jnp.zeros((blk, 1), F32)
    init_a = jnp.zeros((blk, 2 * dh), F32)
    m0, l0, a0, m1, l1, a1 = lax.fori_loop(0, qi + 1, body, (init_m, init_l, init_a, init_m, init_l, init_a))
    o = a0 / l0 - _lam(lam_ref, lam_init) * (a1 / l1)
    o_ref[...] = _bf(_subln(o, sg_ref[...], lam_init))


def _diff_attn_prompt(q, k, v, lam_b, subln_b, j, lam_init, nb, t, nh):
    m = q.shape[0]
    dh = q.shape[1] // (2 * nh)
    blk = _tile(t, 512, 128)
    nq = t // blk
    return pl.pallas_call(
        functools.partial(_attn_kernel, blk=blk, dh=dh, lam_init=lam_init),
        grid=(nb, nh, nq),
        in_specs=[pl.BlockSpec((blk, 2 * dh), lambda b, h, i: (b * nq + i, h)),
                  pl.BlockSpec((t, 2 * dh), lambda b, h, i: (b, h)),
                  pl.BlockSpec((t, 2 * dh), lambda b, h, i: (b, h)),
                  pl.BlockSpec((None, 4, dh), lambda b, h, i: (j, 0, 0)),
                  pl.BlockSpec((None, 1, 2 * dh), lambda b, h, i: (j, 0, 0))],
        out_specs=pl.BlockSpec((blk, 2 * dh), lambda b, h, i: (b * nq + i, h)),
        out_shape=jax.ShapeDtypeStruct((m, nh * 2 * dh), BF16),
        compiler_params=_cparams(("arbitrary", "arbitrary", "arbitrary")),
        name="diff_attn_prompt",
    )(q, k, v, lam_b, subln_b.reshape(subln_b.shape[0], 1, 2 * dh))


def _decode_kernel(pt_ref, wq_ref, *refs, pp, nh, dh, page, lam_init):
    k_refs = refs[:pp]
    v_refs = refs[pp:2 * pp]
    kn_ref, vn_ref, lam_ref, sg_ref, o_ref, st_ref = refs[2 * pp:]
    p = pl.program_id(1)
    last = pl.num_programs(1) - 1
    w = 2 * dh

    @pl.when(p == 0)
    def _():
        st_ref[:, 0] = jnp.full((nh, 8, w), NEG, F32)
        st_ref[:, 1:] = jnp.zeros((nh, 3, 8, w), F32)

    def update(h, s, vh):
        r = s.shape[0]
        m, l, a0, a1 = st_ref[h, 0], st_ref[h, 1], st_ref[h, 2], st_ref[h, 3]
        m_new = jnp.maximum(m, jnp.max(s.reshape(r // 8, 8, w), axis=0))
        alpha = jnp.exp(m - m_new)
        pr = jnp.exp(s.reshape(r // 8, 8, w) - m_new[None])
        p0 = jnp.concatenate([pr[:, :, :dh], pr[:, :, :dh]], axis=-1)
        p1 = jnp.concatenate([pr[:, :, dh:], pr[:, :, dh:]], axis=-1)
        v3 = vh.reshape(r // 8, 8, w)
        a0f = jnp.concatenate([alpha[:, :dh], alpha[:, :dh]], axis=-1)
        a1f = jnp.concatenate([alpha[:, dh:], alpha[:, dh:]], axis=-1)
        st_ref[h, 0] = m_new
        st_ref[h, 1] = alpha * l + jnp.sum(pr, axis=0)
        st_ref[h, 2] = a0f * a0 + jnp.sum(p0 * v3, axis=0)
        st_ref[h, 3] = a1f * a1 + jnp.sum(p1 * v3, axis=0)

    for h in range(nh):
        wq = wq_ref[h]
        for i in range(pp):
            kh = _bf(k_refs[i][:, h * w:(h + 1) * w])
            update(h, _dot(kh, wq), v_refs[i][:, h * w:(h + 1) * w])

    @pl.when(p == last)
    def _():
        lam = _lam(lam_ref, lam_init)
        slot = lax.broadcasted_iota(I32, (8, w), 0)
        for h in range(nh):
            kn = jnp.broadcast_to(kn_ref[:, h * w:(h + 1) * w], (8, w))
            vn = jnp.broadcast_to(vn_ref[:, h * w:(h + 1) * w], (8, w))
            s = jnp.where(slot == 0, _dot(_bf(kn), wq_ref[h]), NEG)
            update(h, s, vn)
            m, l, a0, a1 = st_ref[h, 0], st_ref[h, 1], st_ref[h, 2], st_ref[h, 3]
            mt = jnp.max(m, axis=0, keepdims=True)
            sc = jnp.exp(m - mt)
            lt = jnp.sum(sc * l, axis=0, keepdims=True)
            sc0 = jnp.concatenate([sc[:, :dh], sc[:, :dh]], axis=-1)
            sc1 = jnp.concatenate([sc[:, dh:], sc[:, dh:]], axis=-1)
            n0 = jnp.sum(sc0 * a0, axis=0, keepdims=True)
            n1 = jnp.sum(sc1 * a1, axis=0, keepdims=True)
            l0 = jnp.concatenate([lt[:, :dh], lt[:, :dh]], axis=-1)
            l1 = jnp.concatenate([lt[:, dh:], lt[:, dh:]], axis=-1)
            o = n0 / l0 - lam * (n1 / l1)
            o_ref[:, h * w:(h + 1) * w] = _bf(_subln(o, sg_ref[...], lam_init))


def _diff_attn_decode(q, k_new, v_new, cache_k, cache_v, page_table, lam_b, subln_b, j, lam_init, nh):
    nb = q.shape[0]
    dh = q.shape[1] // (2 * nh)
    w = 2 * dh
    n_pool, page = cache_k.shape[0], cache_k.shape[1]
    npages = page_table.shape[1]
    pp = 4 if npages % 4 == 0 else 1
    qs = _bf(q.astype(F32).reshape(nb, nh, 2, dh) * (dh ** -0.5))
    eye = jnp.eye(2, dtype=BF16)
    wq = (qs[:, :, :, :, None, None] * eye[None, None, :, None, :, None])
    wq = jnp.broadcast_to(wq, (nb, nh, 2, dh, 2, dh)).reshape(nb, nh, w, w)
    ck = cache_k.reshape(n_pool, page, nh * w)
    cv = cache_v.reshape(n_pool, page, nh * w)

    def page_spec(i):
        return pl.BlockSpec((None, page, nh * w), lambda b, p, pt: (pt[b * npages + p * pp + i], 0, 0))

    grid_spec = pltpu.PrefetchScalarGridSpec(
        num_scalar_prefetch=1,
        grid=(nb, npages // pp),
        in_specs=([pl.BlockSpec((None, nh, w, w), lambda b, p, pt: (b, 0, 0, 0))]
                  + [page_spec(i) for i in range(pp)] * 2
                  + [pl.BlockSpec((None, 1, nh * w), lambda b, p, pt: (b, 0, 0)),
                     pl.BlockSpec((None, 1, nh * w), lambda b, p, pt: (b, 0, 0)),
                     pl.BlockSpec((None, 4, dh), lambda b, p, pt: (j, 0, 0)),
                     pl.BlockSpec((None, 1, w), lambda b, p, pt: (j, 0, 0))]),
        out_specs=pl.BlockSpec((None, 1, nh * w), lambda b, p, pt: (b, 0, 0)),
        scratch_shapes=[pltpu.VMEM((nh, 4, 8, w), F32)],
    )
    out = pl.pallas_call(
        functools.partial(_decode_kernel, pp=pp, nh=nh, dh=dh, page=page, lam_init=lam_init),
        grid_spec=grid_spec,
        out_shape=jax.ShapeDtypeStruct((nb, 1, nh * w), BF16),
        compiler_params=_cparams(("arbitrary", "arbitrary")),
        name="diff_attn_decode",
    )(page_table.reshape(-1), wq, *([ck] * pp), *([cv] * pp),
      k_new.reshape(nb, 1, nh * w), v_new.reshape(nb, 1, nh * w), lam_b,
      subln_b.reshape(subln_b.shape[0], 1, w))
    return out.reshape(nb, nh * w)


def _final_kernel(x_ref, g_ref, sh_ref, sc_ref, o_ref, *, grp):
    i = pl.program_id(0)
    o_ref[...] = _norm_mod(x_ref[...], g_ref[...], _mod_row(sc_ref, grp, i), _mod_row(sh_ref, grp, i))


def _final_norm(grp, x, gain, mod):
    m, d = x.shape
    bm = min(grp.bm, 512)
    g2 = _Group(grp.m, grp.seq_len, bm, grp.per_row, grp.row_block)
    return pl.pallas_call(
        functools.partial(_final_kernel, grp=g2),
        grid=(m // bm,),
        in_specs=[pl.BlockSpec((bm, d), lambda i: (i, 0)),
                  pl.BlockSpec((1, d), lambda i: (0, 0)),
                  _mod_spec(g2, mod[1], 0, d, 1),
                  _mod_spec(g2, mod[1], 1, d, 1)],
        out_specs=pl.BlockSpec((bm, d), lambda i: (i, 0)),
        out_shape=jax.ShapeDtypeStruct((m, d), F32),
        compiler_params=_cparams(("arbitrary",)),
        name="final_norm",
    )(x, gain.reshape(1, d), mod[0], mod[0])


def _rope_tables(positions, d):
    inv = ROPE_BASE ** (-np.arange(0, d, 2, dtype=np.float64) / d)
    ang = np.asarray(positions, np.float64)[:, None] * inv[None, :]
    return jnp.asarray(np.cos(ang), F32), jnp.asarray(np.sin(ang), F32)


def _trunk(grp, x, mods, p, dims, mixer):
    depth, n_a, nh_a, nh_b, dk_a, dv_a, dh_b = dims
    mod, mod_kv, mod_f = mods
    m, d = x.shape
    states = []
    k_sh = v_sh = None
    for l in range(depth):
        if l < n_a:
            q, k, v, g = _nm_matmul(
                grp, x, p["norm_mix"], l, (mod, l), 0, 1, p["w_ret_in"], l,
                [(nh_a * dk_a, BF16, "rope", 1.0), (nh_a * dk_a, F32, "rope", dk_a ** -0.5),
                 (nh_a * dv_a, BF16, "plain", 1.0), (nh_a * dv_a, F32, "plain", 1.0)],
                rope_tabs=mixer["rope"])
            o, s_new = mixer["retention"](l, q, k, v, g)
            states.append(s_new)
            x = _matmul_resid(grp, o, p["w_ret_out"], l, x, (mod, l), 2)
        else:
            j = l - n_a
            lam_init = 0.8 - 0.6 * math.exp(-0.3 * l)
            (q,) = _nm_matmul(grp, x, p["norm_mix"], l, (mod, l), 0, 1, p["w_q_b"], j,
                              [(nh_b * 2 * dh_b, BF16, "plain", 1.0)])
            o = mixer["attention"](j, lam_init, q, k_sh, v_sh)
            x = _matmul_resid(grp, o, p["w_o_b"], j, x, (mod, l), 2)
        i = l // 2
        if l % 2 == 0:
            x = _ffn_dense(grp, x, p["norm_ffn"], l, (mod, l),
                           p["w_ffn_gate"], p["w_ffn_up"], p["w_ffn_down"], i)
        else:
            x = _moe(grp, x, p["norm_ffn"], l, (mod, l), p["w_router"], p["b_router"],
                     p["w_exp_gate"], p["w_exp_up"], p["w_exp_down"], i, mixer["moe_tile"])
        if l == n_a - 1:
            wb = nh_b * 2 * dh_b
            k_sh, v_sh = _nm_matmul(grp, x, p["norm_kv"].reshape(1, d), 0, (mod_kv, 0), 0, 1,
                                    p["w_kv"].reshape(1, d, 2 * wb), 0,
                                    [(wb, F32, "plain", 1.0), (wb, F32, "plain", 1.0)])
    y = _final_norm(grp, x, p["norm_f"], (mod_f, 0))
    return y, states, k_sh, v_sh


def kernel(x_prompt, x_sample, state_ret, cache_k, cache_v, page_table, c_prompt, c_sample,
           w_ada, b_ada, norm_mix, norm_ffn, w_ret_in, w_ret_out, w_ada_kv, b_ada_kv, norm_kv, w_kv,
           w_q_b, w_o_b, lam_b, subln_b, w_ffn_gate, w_ffn_up, w_ffn_down, w_router, b_router,
           w_exp_gate, w_exp_up, w_exp_down, w_ada_f, b_ada_f, norm_f):
    p = dict(norm_mix=norm_mix, norm_ffn=norm_ffn, w_ret_in=w_ret_in, w_ret_out=w_ret_out,
             norm_kv=norm_kv, w_kv=w_kv, w_q_b=w_q_b, w_o_b=w_o_b, w_ffn_gate=w_ffn_gate,
             w_ffn_up=w_ffn_up, w_ffn_down=w_ffn_down, w_router=w_router, b_router=b_router,
             w_exp_gate=w_exp_gate, w_exp_up=w_exp_up, w_exp_down=w_exp_down, norm_f=norm_f)
    nb, t, d = x_prompt.shape
    db, dt, _ = x_sample.shape
    assert dt == 1 and nb <= MOD_ROWS and db == MOD_ROWS
    depth = w_ada.shape[0]
    n_a = w_ret_in.shape[0]
    nh_a, dk_a, dv_a = state_ret.shape[2], state_ret.shape[3], state_ret.shape[4]
    nh_b = cache_k.shape[2]
    dh_b = cache_k.shape[3] // 2
    page = cache_k.shape[1]
    past_len = page_table.shape[1] * page
    dims = (depth, n_a, nh_a, nh_b, dk_a, dv_a, dh_b)

    c_rows = jnp.concatenate([c_prompt, jnp.zeros((MOD_ROWS - nb, d), F32), c_sample], axis=0)
    mod = _ada(c_rows, w_ada, b_ada, 6)
    mod_kv = _ada(c_rows, w_ada_kv.reshape(1, d, 2 * d), b_ada_kv.reshape(1, 2 * d), 2)
    mod_f = _ada(c_rows, w_ada_f.reshape(1, d, 2 * d), b_ada_f.reshape(1, 2 * d), 2)
    mods = (mod, mod_kv, mod_f)

    gp = _Group(nb * t, t, _tile(t, 1024, 128), False, 0)
    mixer_p = dict(
        rope=_rope_tables(np.arange(t), dk_a),
        retention=lambda l, q, k, v, g: _retention_prompt(q, k, v, g, nb, t, nh_a),
        attention=lambda j, li, q, k, v: _diff_attn_prompt(q, k, v, lam_b, subln_b, j, li, nb, t, nh_b),
        moe_tile=_tile(TOP_K * nb * t, 1024, 128))
    y_p, st_p, k_p, v_p = _trunk(gp, x_prompt.reshape(nb * t, d), mods, p, dims, mixer_p)

    gs = _Group(db, db, db, True, 1)
    mixer_s = dict(
        rope=_rope_tables(np.full((db,), past_len), dk_a),
        retention=lambda l, q, k, v, g: _retention_step(q, k, v, g, state_ret[l], nh_a),
        attention=lambda j, li, q, k, v: _diff_attn_decode(
            q, k, v, cache_k, cache_v, page_table, lam_b, subln_b, j, li, nh_b),
        moe_tile=db)
    y_s, st_s, k_s, v_s = _trunk(gs, x_sample.reshape(db, d), mods, p, dims, mixer_s)

    return (y_p.reshape(nb, t, d), y_s.reshape(db, 1, d),
            jnp.stack(st_p), jnp.stack(st_s),
            k_p.reshape(nb, t, nh_b, 2 * dh_b), v_p.reshape(nb, t, nh_b, 2 * dh_b),
            k_s.reshape(db, 1, nh_b, 2 * dh_b), v_s.reshape(db, 1, nh_b, 2 * dh_b))
```

```python
import functools
import math

import numpy as np
import jax
import jax.numpy as jnp
from jax import lax
from jax.experimental import pallas as pl
from jax.experimental.pallas import tpu as pltpu

F32 = jnp.float32
BF16 = jnp.bfloat16
I32 = jnp.int32
EPS = 1e-6
RET_CHUNK = 128
ROPE_BASE = 10000.0
TOP_K = 2
NEG = -1e30
MOD_ROWS = 8
VMEM_LIMIT = 56 << 20


def _cparams(sem):
    return pltpu.CompilerParams(dimension_semantics=sem, vmem_limit_bytes=VMEM_LIMIT)


def _tile(n, pref, mult=8):
    if n <= pref:
        return n
    for t in range(pref, 0, -1):
        if n % t == 0 and t % mult == 0:
            return t
    return n


def _bf(x):
    return x.astype(BF16)


def _dot(a, b):
    return jnp.dot(a, b, preferred_element_type=F32)


def _dot_nt(a, b):
    return lax.dot_general(a, b, (((1,), (1,)), ((), ())), preferred_element_type=F32)


def _dot_tn(a, b):
    return lax.dot_general(a, b, (((0,), (0,)), ((), ())), preferred_element_type=F32)


def _norm_mod(x, gain, scale, shift):
    y = x * lax.rsqrt(jnp.mean(x * x, axis=-1, keepdims=True) + EPS)
    return (y * gain) * (1.0 + scale) + shift


class _Group:
    def __init__(self, m, seq_len, bm, per_row, row_block):
        self.m = m
        self.seq_len = seq_len
        self.bm = bm
        self.per_row = per_row
        self.row_block = row_block
        self.tps = max(seq_len // bm, 1)


def _mod_spec(grp, layer, chunk, d, grid_rank):
    idx = (layer, chunk, grp.row_block, 0)
    if grid_rank == 1:
        return pl.BlockSpec((None, None, MOD_ROWS, d), lambda i: idx)
    return pl.BlockSpec((None, None, MOD_ROWS, d), lambda i, j: idx)


def _mod_row(ref, grp, i):
    if grp.per_row:
        return ref[...]
    return ref[pl.ds(i // grp.tps, 1), :]


def _ada_kernel(c_ref, w_ref, b_ref, o_ref):
    a = _bf(jax.nn.silu(c_ref[...]))
    o_ref[...] = _dot(a, _bf(w_ref[...])) + b_ref[...]


def _ada(c_rows, w, b, nch):
    L, d, n = w.shape
    rows = c_rows.shape[0]
    bn = _tile(d, 1024, 128)
    nj = d // bn
    return pl.pallas_call(
        _ada_kernel,
        grid=(L, nch, nj),
        in_specs=[pl.BlockSpec((rows, d), lambda l, c, j: (0, 0)),
                  pl.BlockSpec((None, d, bn), lambda l, c, j: (l, 0, c * nj + j)),
                  pl.BlockSpec((None, 1, bn), lambda l, c, j: (l, 0, c * nj + j))],
        out_specs=pl.BlockSpec((None, None, rows, bn), lambda l, c, j: (l, c, 0, j)),
        out_shape=jax.ShapeDtypeStruct((L, nch, rows, d), F32),
        compiler_params=_cparams(("arbitrary", "arbitrary", "arbitrary")),
        name="ada",
    )(c_rows, w, b.reshape(L, 1, n))


def _nm_kernel(*refs, grp, segs, bn, rope):
    x_ref, g_ref, sh_ref, sc_ref, w_ref = refs[:5]
    k = 5
    if rope:
        cos_ref, sin_ref = refs[5:7]
        k = 7
    outs = refs[k:k + len(segs)]
    hs_ref = refs[k + len(segs)]
    i = pl.program_id(0)
    j = pl.program_id(1)

    @pl.when(j == 0)
    def _():
        h = _norm_mod(x_ref[...], g_ref[...], _mod_row(sc_ref, grp, i), _mod_row(sh_ref, grp, i))
        hs_ref[...] = _bf(h)

    acc = _dot(hs_ref[...], _bf(w_ref[...]))
    t0 = 0
    for (width, dtype, kind, scale), o_ref in zip(segs, outs):
        nt = width // bn

        def _store(o_ref=o_ref, dtype=dtype, kind=kind, scale=scale):
            if kind == "rope":
                half = 128
                parts = []
                for hh in range(bn // (2 * half)):
                    x1 = acc[:, hh * 2 * half:hh * 2 * half + half]
                    x2 = acc[:, hh * 2 * half + half:(hh + 1) * 2 * half]
                    c = cos_ref[...]
                    s = sin_ref[...]
                    parts.append((x1 * c - x2 * s) * scale)
                    parts.append((x2 * c + x1 * s) * scale)
                o_ref[...] = jnp.concatenate(parts, axis=-1).astype(dtype)
            else:
                o_ref[...] = acc.astype(dtype)

        pl.when((j >= t0) & (j < t0 + nt))(_store)
        t0 += nt


def _nm_matmul(grp, x, gain, layer, mod, ch_shift, ch_scale, w, wl, segs, rope_tabs=None, bn=512):
    m, d = x.shape
    n = w.shape[-1]
    bm = grp.bm
    bn = _tile(n, bn, 256)
    assert all(s[0] % bn == 0 for s in segs) and sum(s[0] for s in segs) == n
    nj = n // bn
    in_specs = [pl.BlockSpec((bm, d), lambda i, j: (i, 0)),
                pl.BlockSpec((None, 1, d), lambda i, j: (layer, 0, 0)),
                _mod_spec(grp, mod[1], ch_shift, d, 2),
                _mod_spec(grp, mod[1], ch_scale, d, 2),
                pl.BlockSpec((None, d, bn), lambda i, j: (wl, 0, j))]
    args = [x, gain.reshape(gain.shape[0], 1, d), mod[0], mod[0], w]
    if rope_tabs is not None:
        tps = grp.tps
        rows = rope_tabs[0].shape[0]
        rb = min(bm, rows)
        in_specs += [pl.BlockSpec((rb, 128), lambda i, j: (i % tps, 0))] * 2
        args += list(rope_tabs)
    out_specs, out_shape, t0 = [], [], 0
    for width, dtype, _, _ in segs:
        nt = width // bn
        out_specs.append(pl.BlockSpec(
            (bm, bn), lambda i, j, t0=t0, nt=nt: (i, jnp.clip(j - t0, 0, nt - 1))))
        out_shape.append(jax.ShapeDtypeStruct((m, width), dtype))
        t0 += nt
    return pl.pallas_call(
        functools.partial(_nm_kernel, grp=grp, segs=segs, bn=bn, rope=rope_tabs is not None),
        grid=(m // bm, nj),
        in_specs=in_specs,
        out_specs=out_specs,
        out_shape=out_shape,
        scratch_shapes=[pltpu.VMEM((bm, d), BF16)],
        compiler_params=_cparams(("arbitrary", "arbitrary")),
        name="norm_matmul",
    )(*args)


def _mr_kernel(a_ref, w_ref, r_ref, gt_ref, o_ref, *, grp):
    i = pl.program_id(0)
    acc = _dot(a_ref[...], _bf(w_ref[...]))
    o_ref[...] = r_ref[...] + _mod_row(gt_ref, grp, i) * acc


def _matmul_resid(grp, a, w, wl, resid, mod, ch_gate, bn=512):
    m, k = a.shape
    n = w.shape[-1]
    bm = grp.bm
    bn = _tile(n, bn, 128)
    return pl.pallas_call(
        functools.partial(_mr_kernel, grp=grp),
        grid=(m // bm, n // bn),
        in_specs=[pl.BlockSpec((bm, k), lambda i, j: (i, 0)),
                  pl.BlockSpec((None, k, bn), lambda i, j: (wl, 0, j)),
                  pl.BlockSpec((bm, bn), lambda i, j: (i, j)),
                  pl.BlockSpec((None, None, MOD_ROWS, bn),
                               lambda i, j: (mod[1], ch_gate, grp.row_block, j))],
        out_specs=pl.BlockSpec((bm, bn), lambda i, j: (i, j)),
        out_shape=jax.ShapeDtypeStruct((m, n), F32),
        compiler_params=_cparams(("arbitrary", "arbitrary")),
        name="matmul_resid",
    )(a, w, resid, mod[0])


def _ffn_kernel(x_ref, g_ref, sh_ref, sc_ref, gt_ref, wg_ref, wu_ref, wd_ref, o_ref, hs_ref, *, grp, nf):
    i = pl.program_id(0)
    j = pl.program_id(1)

    @pl.when(j == 0)
    def _():
        h = _norm_mod(x_ref[...], g_ref[...], _mod_row(sc_ref, grp, i), _mod_row(sh_ref, grp, i))
        hs_ref[...] = _bf(h)
        o_ref[...] = jnp.zeros_like(o_ref)

    hs = hs_ref[...]
    act = _bf(jax.nn.silu(_dot(hs, _bf(wg_ref[...]))) * _dot(hs, _bf(wu_ref[...])))
    o_ref[...] += _dot(act, _bf(wd_ref[...]))

    @pl.when(j == nf - 1)
    def _():
        o_ref[...] = x_ref[...] + _mod_row(gt_ref, grp, i) * o_ref[...]


def _ffn_dense(grp, x, gain, layer, mod, wg, wu, wd, wl):
    m, d = x.shape
    f = wg.shape[-1]
    bm = grp.bm
    bf = _tile(f, 256, 128)
    nf = f // bf
    return pl.pallas_call(
        functools.partial(_ffn_kernel, grp=grp, nf=nf),
        grid=(m // bm, nf),
        in_specs=[pl.BlockSpec((bm, d), lambda i, j: (i, 0), pipeline_mode=pl.Buffered(1)),
                  pl.BlockSpec((None, 1, d), lambda i, j: (layer, 0, 0)),
                  _mod_spec(grp, mod[1], 3, d, 2),
                  _mod_spec(grp, mod[1], 4, d, 2),
                  _mod_spec(grp, mod[1], 5, d, 2),
                  pl.BlockSpec((None, d, bf), lambda i, j: (wl, 0, j)),
                  pl.BlockSpec((None, d, bf), lambda i, j: (wl, 0, j)),
                  pl.BlockSpec((None, bf, d), lambda i, j: (wl, j, 0))],
        out_specs=pl.BlockSpec((bm, d), lambda i, j: (i, 0)),
        out_shape=jax.ShapeDtypeStruct((m, d), F32),
        scratch_shapes=[pltpu.VMEM((bm, d), BF16)],
        compiler_params=_cparams(("arbitrary", "arbitrary")),
        name="ffn_dense",
    )(x, gain.reshape(gain.shape[0], 1, d), mod[0], mod[0], mod[0], wg, wu, wd)


def _router_kernel(x_ref, g_ref, sh_ref, sc_ref, wr_ref, br_ref, h_ref, idx_ref, gw_ref, *, grp):
    i = pl.program_id(0)
    h = _norm_mod(x_ref[...], g_ref[...], _mod_row(sc_ref, grp, i), _mod_row(sh_ref, grp, i))
    h_ref[...] = h
    logits = jnp.dot(h, wr_ref[...], precision=lax.Precision.HIGHEST,
                     preferred_element_type=F32) + br_ref[...]
    lane = lax.broadcasted_iota(I32, logits.shape, 1).astype(F32)
    big = float(logits.shape[-1])
    v0 = jnp.max(logits, axis=-1, keepdims=True)
    i0 = jnp.min(jnp.where(logits == v0, lane, big), axis=-1, keepdims=True)
    rest = jnp.where(lane == i0, -jnp.inf, logits)
    v1 = jnp.max(rest, axis=-1, keepdims=True)
    i1 = jnp.min(jnp.where(rest == v1, lane, big), axis=-1, keepdims=True)
    e = jnp.exp(v1 - v0)
    w0 = 1.0 / (1.0 + e)
    w1 = e / (1.0 + e)
    idx_ref[...] = jnp.where(lane == 0.0, i0, jnp.where(lane == 1.0, i1, 0.0)).astype(I32)
    gw_ref[...] = jnp.where(lane == 0.0, w0, jnp.where(lane == 1.0, w1, 0.0))


def _router(grp, x, gain, layer, mod, w_router, b_router, wl):
    m, d = x.shape
    ne = 128
    pad = ne - w_router.shape[-1]
    w_router = jnp.pad(w_router, ((0, 0), (0, 0), (0, pad)))
    b_router = jnp.pad(b_router, ((0, 0), (0, pad)), constant_values=NEG)
    bm = min(grp.bm, 512)
    g2 = _Group(grp.m, grp.seq_len, bm, grp.per_row, grp.row_block)
    return pl.pallas_call(
        functools.partial(_router_kernel, grp=g2),
        grid=(m // bm,),
        in_specs=[pl.BlockSpec((bm, d), lambda i: (i, 0)),
                  pl.BlockSpec((None, 1, d), lambda i: (layer, 0, 0)),
                  _mod_spec(g2, mod[1], 3, d, 1),
                  _mod_spec(g2, mod[1], 4, d, 1),
                  pl.BlockSpec((None, d, ne), lambda i: (wl, 0, 0)),
                  pl.BlockSpec((None, 1, ne), lambda i: (wl, 0, 0))],
        out_specs=[pl.BlockSpec((bm, d), lambda i: (i, 0)),
                   pl.BlockSpec((bm, 128), lambda i: (i, 0)),
                   pl.BlockSpec((bm, 128), lambda i: (i, 0))],
        out_shape=[jax.ShapeDtypeStruct((m, d), F32),
                   jax.ShapeDtypeStruct((m, 128), I32),
                   jax.ShapeDtypeStruct((m, 128), F32)],
        compiler_params=_cparams(("arbitrary",)),
        name="router",
    )(x, gain.reshape(gain.shape[0], 1, d), mod[0], mod[0], w_router,
      b_router.reshape(b_router.shape[0], 1, ne))


def _gather_rows(idx_ref, n, src_hbm, dst, sem):
    def issue(r, c):
        pltpu.make_async_copy(src_hbm.at[pl.ds(idx_ref[0, r], 1)], dst.at[pl.ds(r, 1)], sem).start()
        return c
    lax.fori_loop(0, n, issue, 0, unroll=8)


def _wait_rows(n, src_hbm, dst, sem):
    def wait(r, c):
        pltpu.make_async_copy(src_hbm.at[pl.ds(0, 1)], dst.at[pl.ds(0, 1)], sem).wait()
        return c
    lax.fori_loop(0, n, wait, 0, unroll=8)


def _moe_kernel(te_ref, nv_ref, tok_ref, h_hbm, wg_ref, wu_ref, wd_ref, o_ref, xg_ref, hs_ref, sem, *, bm):
    t = pl.program_id(0)
    j = pl.program_id(1)
    valid = t < nv_ref[0]

    @pl.when(valid & (j == 0))
    def _():
        _gather_rows(tok_ref, bm, h_hbm, xg_ref, sem)
        _wait_rows(bm, h_hbm, xg_ref, sem)
        hs_ref[...] = _bf(xg_ref[...])

    @pl.when(j == 0)
    def _():
        o_ref[...] = jnp.zeros_like(o_ref)

    @pl.when(valid)
    def _():
        hs = hs_ref[...]
        act = _bf(jax.nn.silu(_dot(hs, _bf(wg_ref[...]))) * _dot(hs, _bf(wu_ref[...])))
        o_ref[...] += _dot(act, _bf(wd_ref[...]))


def _moe_experts(h, tok_sorted, tile_expert, n_valid, wg, wu, wd, wl, bm):
    m, d = h.shape
    f = wg.shape[-1]
    nt = tile_expert.shape[0]
    bf = _tile(f, 256, 128)
    nf = f // bf

    def fj(t, j, nv):
        return jnp.where(t < nv[0], j, nf - 1)

    grid_spec = pltpu.PrefetchScalarGridSpec(
        num_scalar_prefetch=2,
        grid=(nt, nf),
        in_specs=[pl.BlockSpec((None, 1, bm), lambda t, j, te, nv: (t, 0, 0), memory_space=pltpu.SMEM),
                  pl.BlockSpec(memory_space=pl.ANY),
                  pl.BlockSpec((None, None, d, bf), lambda t, j, te, nv: (wl, te[t], 0, fj(t, j, nv))),
                  pl.BlockSpec((None, None, d, bf), lambda t, j, te, nv: (wl, te[t], 0, fj(t, j, nv))),
                  pl.BlockSpec((None, None, bf, d), lambda t, j, te, nv: (wl, te[t], fj(t, j, nv), 0))],
        out_specs=pl.BlockSpec((bm, d), lambda t, j, te, nv: (t, 0)),
        scratch_shapes=[pltpu.VMEM((bm, d), F32), pltpu.VMEM((bm, d), BF16), pltpu.SemaphoreType.DMA(())],
    )
    return pl.pallas_call(
        functools.partial(_moe_kernel, bm=bm),
        grid_spec=grid_spec,
        out_shape=jax.ShapeDtypeStruct((nt * bm, d), F32),
        compiler_params=_cparams(("arbitrary", "arbitrary")),
        name="moe_experts",
    )(tile_expert, n_valid, tok_sorted.reshape(nt, 1, bm), h, wg, wu, wd)


def _combine_kernel(p0_ref, p1_ref, ys_hbm, x_ref, gw_ref, gt_ref, o_ref, ya_ref, yb_ref, sem, *, grp, bm):
    i = pl.program_id(0)
    _gather_rows(p0_ref, bm, ys_hbm, ya_ref, sem)
    _gather_rows(p1_ref, bm, ys_hbm, yb_ref, sem)
    _wait_rows(2 * bm, ys_hbm, ya_ref, sem)
    gw = gw_ref[...]
    y = gw[:, 0:1] * ya_ref[...] + gw[:, 1:2] * yb_ref[...]
    o_ref[...] = x_ref[...] + _mod_row(gt_ref, grp, i) * y


def _moe_combine(grp, x, ys, pos, gw, mod):
    m, d = x.shape
    bm = min(grp.bm, 256)
    g2 = _Group(grp.m, grp.seq_len, bm, grp.per_row, grp.row_block)
    nt = m // bm
    p0 = pos[:, 0].reshape(nt, 1, bm)
    p1 = pos[:, 1].reshape(nt, 1, bm)
    return pl.pallas_call(
        functools.partial(_combine_kernel, grp=g2, bm=bm),
        grid=(nt,),
        in_specs=[pl.BlockSpec((None, 1, bm), lambda i: (i, 0, 0), memory_space=pltpu.SMEM),
                  pl.BlockSpec((None, 1, bm), lambda i: (i, 0, 0), memory_space=pltpu.SMEM),
                  pl.BlockSpec(memory_space=pl.ANY),
                  pl.BlockSpec((bm, d), lambda i: (i, 0)),
                  pl.BlockSpec((bm, 128), lambda i: (i, 0)),
                  _mod_spec(g2, mod[1], 5, d, 1)],
        out_specs=pl.BlockSpec((bm, d), lambda i: (i, 0)),
        out_shape=jax.ShapeDtypeStruct((m, d), F32),
        scratch_shapes=[pltpu.VMEM((bm, d), F32), pltpu.VMEM((bm, d), F32), pltpu.SemaphoreType.DMA(())],
        compiler_params=_cparams(("arbitrary",)),
        name="moe_combine",
    )(p0, p1, ys, x, gw, mod[0])


def _moe(grp, x, gain, layer, mod, w_router, b_router, wg, wu, wd, wl, bm_tile):
    m, d = x.shape
    ne = w_router.shape[-1]
    h, idx, gw = _router(grp, x, gain, layer, mod, w_router, b_router, wl)
    e_flat = idx[:, :TOP_K].reshape(-1)
    onehot = (e_flat[:, None] == jnp.arange(ne, dtype=I32)[None, :]).astype(I32)
    csum = jnp.cumsum(onehot, axis=0)
    rank = jnp.take_along_axis(csum, e_flat[:, None], axis=1)[:, 0] - 1
    counts = csum[-1]
    tiles_e = (counts + bm_tile - 1) // bm_tile
    tile_end = jnp.cumsum(tiles_e)
    pos = (tile_end - tiles_e)[e_flat] * bm_tile + rank
    n_valid = tile_end[-1:]
    nt = (TOP_K * m) // bm_tile + ne
    tile_ids = jnp.arange(nt, dtype=I32)
    tile_expert = jnp.minimum(jnp.sum((tile_end[None, :] <= tile_ids[:, None]).astype(I32), axis=1), ne - 1)
    tile_expert = jnp.where(jnp.arange(nt) < n_valid[0], tile_expert, tile_expert[jnp.maximum(n_valid[0] - 1, 0)])
    tok_sorted = jnp.zeros((nt * bm_tile,), I32).at[pos].set(jnp.arange(TOP_K * m, dtype=I32) // TOP_K)
    ys = _moe_experts(h, tok_sorted, tile_expert, n_valid.astype(I32), wg, wu, wd, wl, bm_tile)
    return _moe_combine(grp, x, ys, pos.reshape(m, TOP_K).astype(I32), gw, mod)


def _ret_log_gamma(nh):
    return np.log(1.0 - 2.0 ** (-5.0 - np.arange(nh, dtype=np.float64)))


def _groupnorm(o):
    mu = jnp.mean(o, axis=-1, keepdims=True)
    dlt = o - mu
    var = jnp.mean(dlt * dlt, axis=-1, keepdims=True)
    return dlt * lax.rsqrt(var + EPS)


def _ret_kernel(q_ref, k_ref, v_ref, g_ref, dec_ref, qd_ref, kd_ref, o_ref, s_ref, *, nh, dk, dv, gamma_c):
    c = pl.program_id(1)

    @pl.when(c == 0)
    def _():
        s_ref[...] = jnp.zeros_like(s_ref)

    for h in range(nh):
        q = q_ref[:, h * dk:(h + 1) * dk]
        kf = k_ref[:, h * dk:(h + 1) * dk]
        v = v_ref[:, h * dv:(h + 1) * dv]
        st = s_ref[h]
        inner = _bf(_dot_nt(q, _bf(kf)) * dec_ref[h])
        o = _dot(inner, v) + _dot(q, _bf(st)) * qd_ref[h]
        kv = _dot_tn(_bf(kf * kd_ref[h]), v)
        s_ref[h] = st * gamma_c[h] + kv
        gt = g_ref[:, h * dv:(h + 1) * dv]
        o_ref[:, h * dv:(h + 1) * dv] = _bf(jax.nn.silu(gt) * _groupnorm(o))


def _retention_prompt(q, k, v, g, nb, t, nh):
    m = q.shape[0]
    dk = q.shape[1] // nh
    dv = v.shape[1] // nh
    c = RET_CHUNK if t % RET_CHUNK == 0 else t
    nc = t // c
    lg = _ret_log_gamma(nh)
    n = np.arange(c, dtype=np.float64)
    diff = n[:, None] - n[None, :]
    decay = np.where(diff >= 0, np.exp(lg[:, None, None] * np.maximum(diff, 0.0)), 0.0)
    qd = np.broadcast_to(np.exp(lg[:, None] * (n[None, :] + 1.0))[:, :, None], (nh, c, dv))
    kd = np.broadcast_to(np.exp(lg[:, None] * (c - 1.0 - n[None, :]))[:, :, None], (nh, c, dk))
    gamma_c = tuple(float(np.float32(np.exp(lg[h] * c))) for h in range(nh))
    out, state = pl.pallas_call(
        functools.partial(_ret_kernel, nh=nh, dk=dk, dv=dv, gamma_c=gamma_c),
        grid=(nb, nc),
        in_specs=[pl.BlockSpec((c, nh * dk), lambda b, i: (b * nc + i, 0)),
                  pl.BlockSpec((c, nh * dk), lambda b, i: (b * nc + i, 0)),
                  pl.BlockSpec((c, nh * dv), lambda b, i: (b * nc + i, 0)),
                  pl.BlockSpec((c, nh * dv), lambda b, i: (b * nc + i, 0)),
                  pl.BlockSpec((nh, c, c), lambda b, i: (0, 0, 0)),
                  pl.BlockSpec((nh, c, dv), lambda b, i: (0, 0, 0)),
                  pl.BlockSpec((nh, c, dk), lambda b, i: (0, 0, 0))],
        out_specs=[pl.BlockSpec((c, nh * dv), lambda b, i: (b * nc + i, 0)),
                   pl.BlockSpec((None, nh, dk, dv), lambda b, i: (b, 0, 0, 0))],
        out_shape=[jax.ShapeDtypeStruct((m, nh * dv), BF16),
                   jax.ShapeDtypeStruct((nb, nh, dk, dv), F32)],
        compiler_params=_cparams(("arbitrary", "arbitrary")),
        name="retention_chunk",
    )(q, k, v, g, jnp.asarray(decay, F32), jnp.asarray(qd, F32), jnp.asarray(kd, F32))
    return out, state


def _ret_step_kernel(q_ref, k_ref, v_ref, g_ref, s_ref, o_ref, so_ref, *, nh, dv, gamma):
    for h in range(nh):
        qc = q_ref[h]
        kc = k_ref[h]
        v = v_ref[:, h * dv:(h + 1) * dv].astype(F32)
        st = s_ref[h]
        qk = jnp.sum(qc * kc, axis=0, keepdims=True)
        qs = jnp.sum(st * qc, axis=0, keepdims=True)
        o = qk * v + qs * gamma[h]
        so_ref[h] = st * gamma[h] + kc * v
        gt = g_ref[:, h * dv:(h + 1) * dv]
        o_ref[:, h * dv:(h + 1) * dv] = _bf(jax.nn.silu(gt) * _groupnorm(o))


def _retention_step(q, k, v, g, state, nh):
    nb = q.shape[0]
    dk = q.shape[1] // nh
    dv = v.shape[1] // nh
    lg = _ret_log_gamma(nh)
    gamma = tuple(float(np.float32(np.exp(lg[h]))) for h in range(nh))
    qc = q.astype(F32).reshape(nb, nh, dk, 1)
    kc = k.reshape(nb, nh, dk, 1)
    out, s_new = pl.pallas_call(
        functools.partial(_ret_step_kernel, nh=nh, dv=dv, gamma=gamma),
        grid=(nb,),
        in_specs=[pl.BlockSpec((None, nh, dk, 1), lambda b: (b, 0, 0, 0)),
                  pl.BlockSpec((None, nh, dk, 1), lambda b: (b, 0, 0, 0)),
                  pl.BlockSpec((None, 1, nh * dv), lambda b: (b, 0, 0)),
                  pl.BlockSpec((None, 1, nh * dv), lambda b: (b, 0, 0)),
                  pl.BlockSpec((None, nh, dk, dv), lambda b: (b, 0, 0, 0))],
        out_specs=[pl.BlockSpec((None, 1, nh * dv), lambda b: (b, 0, 0)),
                   pl.BlockSpec((None, nh, dk, dv), lambda b: (b, 0, 0, 0))],
        out_shape=[jax.ShapeDtypeStruct((nb, 1, nh * dv), BF16),
                   jax.ShapeDtypeStruct((nb, nh, dk, dv), F32)],
        compiler_params=_cparams(("arbitrary",)),
        name="retention_step",
    )(qc, kc, v.reshape(nb, 1, nh * dv), g.reshape(nb, 1, nh * dv), state)
    return out.reshape(nb, nh * dv), s_new


def _lam(lam_ref, lam_init):
    lp = lam_ref[...]
    a = jnp.sum(lp[0:1] * lp[1:2], axis=-1, keepdims=True)
    b = jnp.sum(lp[2:3] * lp[3:4], axis=-1, keepdims=True)
    return jnp.exp(a) - jnp.exp(b) + lam_init


def _subln(o, gain, lam_init):
    y = o * lax.rsqrt(jnp.mean(o * o, axis=-1, keepdims=True) + EPS)
    return (y * gain) * (1.0 - lam_init)


def _attn_kernel(q_ref, k_ref, v_ref, lam_ref, sg_ref, o_ref, *, blk, dh, lam_init):
    qi = pl.program_id(2)
    scale = dh ** -0.5
    q = q_ref[...]
    q0, q1 = q[:, :dh], q[:, dh:]
    row = qi * blk + lax.broadcasted_iota(I32, (blk, blk), 0)

    def body(j, carry):
        m0, l0, a0, m1, l1, a1 = carry
        kb = _bf(k_ref[pl.ds(pl.multiple_of(j * blk, blk), blk), :])
        vb = _bf(v_ref[pl.ds(pl.multiple_of(j * blk, blk), blk), :])
        keep = (j * blk + lax.broadcasted_iota(I32, (blk, blk), 1)) <= row

        def upd(qc, kc, m, l, a):
            s = jnp.where(keep, _dot_nt(qc, kc) * scale, NEG)
            m_new = jnp.maximum(m, jnp.max(s, axis=-1, keepdims=True))
            alpha = jnp.exp(m - m_new)
            p = jnp.exp(s - m_new)
            return m_new, alpha * l + jnp.sum(p, axis=-1, keepdims=True), alpha * a + _dot(_bf(p), vb)

        m0, l0, a0 = upd(q0, kb[:, :dh], m0, l0, a0)
        m1, l1, a1 = upd(q1, kb[:, dh:], m1, l1, a1)
        return m0, l0, a0, m1, l1, a1

    init_m = jnp.full((blk, 1), NEG, F32)
    init_l = jnp.zeros((blk, 1), F32)
    init_a = jnp.zeros((blk, 2 * dh), F32)
    m0, l0, a0, m1, l1, a1 = lax.fori_loop(0, qi + 1, body, (init_m, init_l, init_a, init_m, init_l, init_a))
    o = a0 / l0 - _lam(lam_ref, lam_init) * (a1 / l1)
    o_ref[...] = _bf(_subln(o, sg_ref[...], lam_init))


def _diff_attn_prompt(q, k, v, lam_b, subln_b, j, lam_init, nb, t, nh):
    m = q.shape[0]
    dh = q.shape[1] // (2 * nh)
    blk = _tile(t, 512, 128)
    nq = t // blk
    return pl.pallas_call(
        functools.partial(_attn_kernel, blk=blk, dh=dh, lam_init=lam_init),
        grid=(nb, nh, nq),
        in_specs=[pl.BlockSpec((blk, 2 * dh), lambda b, h, i: (b * nq + i, h)),
                  pl.BlockSpec((t, 2 * dh), lambda b, h, i: (b, h)),
                  pl.BlockSpec((t, 2 * dh), lambda b, h, i: (b, h)),
                  pl.BlockSpec((None, 4, dh), lambda b, h, i: (j, 0, 0)),
                  pl.BlockSpec((None, 1, 2 * dh), lambda b, h, i: (j, 0, 0))],
        out_specs=pl.BlockSpec((blk, 2 * dh), lambda b, h, i: (b * nq + i, h)),
        out_shape=jax.ShapeDtypeStruct((m, nh * 2 * dh), BF16),
        compiler_params=_cparams(("arbitrary", "arbitrary", "arbitrary")),
        name="diff_attn_prompt",
    )(q, k, v, lam_b, subln_b.reshape(subln_b.shape[0], 1, 2 * dh))


def _rep(x, c, dh):
    blk = x[..., c * dh:(c + 1) * dh]
    return jnp.concatenate([blk, blk], axis=-1)


def _decode_kernel(pt_ref, q_ref, ones_ref, *refs, pp, dh, lam_init):
    k_refs = refs[:pp]
    v_refs = refs[pp:2 * pp]
    kn_ref, vn_ref, lam_ref, sg_ref, o_ref, st_ref = refs[2 * pp:]
    p = pl.program_id(1)
    last = pl.num_programs(1) - 1
    nh, w = q_ref.shape
    q = q_ref[...].astype(F32) * (dh ** -0.5)

    @pl.when(p == 0)
    def _():
        st_ref[0] = jnp.full((nh, w), NEG, F32)
        st_ref[1:] = jnp.zeros((3, nh, w), F32)

    def scores(k3):
        r = k3.shape[0]
        kq = _bf((k3 * q[None]).reshape(r * nh, w))
        return _dot(kq, ones_ref[...]).reshape(r, nh, w)

    def update(s_list, v_list):
        m, l, a0, a1 = st_ref[0], st_ref[1], st_ref[2], st_ref[3]
        m_new = m
        for s in s_list:
            m_new = jnp.maximum(m_new, jnp.max(s, axis=0))
        alpha = jnp.exp(m - m_new)
        l = alpha * l
        a0 = _rep(alpha, 0, dh) * a0
        a1 = _rep(alpha, 1, dh) * a1
        for s, v in zip(s_list, v_list):
            pr = jnp.exp(s - m_new[None])
            l = l + jnp.sum(pr, axis=0)
            a0 = a0 + jnp.sum(_rep(pr, 0, dh) * v, axis=0)
            a1 = a1 + jnp.sum(_rep(pr, 1, dh) * v, axis=0)
        st_ref[0], st_ref[1], st_ref[2], st_ref[3] = m_new, l, a0, a1

    update([scores(k_refs[i][...]) for i in range(pp)], [v_refs[i][...] for i in range(pp)])

    @pl.when(p == last)
    def _():
        update([scores(kn_ref[...][None])], [vn_ref[...][None]])
        l, a0, a1 = st_ref[1], st_ref[2], st_ref[3]
        o = a0 / _rep(l, 0, dh) - _lam(lam_ref, lam_init) * (a1 / _rep(l, 1, dh))
        o_ref[...] = _bf(_subln(o, sg_ref[...], lam_init))


def _diff_attn_decode(q, k_new, v_new, cache_k, cache_v, page_table, lam_b, subln_b, j, lam_init, nh):
    nb = q.shape[0]
    dh = q.shape[1] // (2 * nh)
    w = 2 * dh
    page = cache_k.shape[1]
    npages = page_table.shape[1]
    pp = 4 if npages % 4 == 0 else 1
    ones = jnp.asarray(np.kron(np.eye(2), np.ones((dh, dh))), BF16)

    def page_spec(i):
        return pl.BlockSpec((None, page, nh, w), lambda b, p, pt: (pt[b * npages + p * pp + i], 0, 0, 0))

    def row_spec():
        return pl.BlockSpec((None, nh, w), lambda b, p, pt: (b, 0, 0))

    grid_spec = pltpu.PrefetchScalarGridSpec(
        num_scalar_prefetch=1,
        grid=(nb, npages // pp),
        in_specs=([row_spec(), pl.BlockSpec((w, w), lambda b, p, pt: (0, 0))]
                  + [page_spec(i) for i in range(pp)] * 2
                  + [row_spec(), row_spec(),
                     pl.BlockSpec((None, 4, dh), lambda b, p, pt: (j, 0, 0)),
                     pl.BlockSpec((None, 1, w), lambda b, p, pt: (j, 0, 0))]),
        out_specs=row_spec(),
        scratch_shapes=[pltpu.VMEM((4, nh, w), F32)],
    )
    out = pl.pallas_call(
        functools.partial(_decode_kernel, pp=pp, dh=dh, lam_init=lam_init),
        grid_spec=grid_spec,
        out_shape=jax.ShapeDtypeStruct((nb, nh, w), BF16),
        compiler_params=_cparams(("arbitrary", "arbitrary")),
        name="diff_attn_decode",
    )(page_table.reshape(-1), q.reshape(nb, nh, w), ones, *([cache_k] * pp), *([cache_v] * pp),
      k_new.reshape(nb, nh, w), v_new.reshape(nb, nh, w), lam_b,
      subln_b.reshape(subln_b.shape[0], 1, w))
    return out.reshape(nb, nh * w)


def _final_kernel(x_ref, g_ref, sh_ref, sc_ref, o_ref, *, grp):
    i = pl.program_id(0)
    o_ref[...] = _norm_mod(x_ref[...], g_ref[...], _mod_row(sc_ref, grp, i), _mod_row(sh_ref, grp, i))


def _final_norm(grp, x, gain, mod):
    m, d = x.shape
    bm = min(grp.bm, 512)
    g2 = _Group(grp.m, grp.seq_len, bm, grp.per_row, grp.row_block)
    return pl.pallas_call(
        functools.partial(_final_kernel, grp=g2),
        grid=(m // bm,),
        in_specs=[pl.BlockSpec((bm, d), lambda i: (i, 0)),
                  pl.BlockSpec((1, d), lambda i: (0, 0)),
                  _mod_spec(g2, mod[1], 0, d, 1),
                  _mod_spec(g2, mod[1], 1, d, 1)],
        out_specs=pl.BlockSpec((bm, d), lambda i: (i, 0)),
        out_shape=jax.ShapeDtypeStruct((m, d), F32),
        compiler_params=_cparams(("arbitrary",)),
        name="final_norm",
    )(x, gain.reshape(1, d), mod[0], mod[0])


def _rope_tables(positions, d):
    inv = ROPE_BASE ** (-np.arange(0, d, 2, dtype=np.float64) / d)
    ang = np.asarray(positions, np.float64)[:, None] * inv[None, :]
    return jnp.asarray(np.cos(ang), F32), jnp.asarray(np.sin(ang), F32)


def _trunk(grp, x, mods, p, dims, mixer):
    depth, n_a, nh_a, nh_b, dk_a, dv_a, dh_b = dims
    mod, mod_kv, mod_f = mods
    m, d = x.shape
    states = []
    k_sh = v_sh = None
    for l in range(depth):
        if l < n_a:
            q, k, v, g = _nm_matmul(
                grp, x, p["norm_mix"], l, (mod, l), 0, 1, p["w_ret_in"], l,
                [(nh_a * dk_a, BF16, "rope", 1.0), (nh_a * dk_a, F32, "rope", dk_a ** -0.5),
                 (nh_a * dv_a, BF16, "plain", 1.0), (nh_a * dv_a, F32, "plain", 1.0)],
                rope_tabs=mixer["rope"])
            o, s_new = mixer["retention"](l, q, k, v, g)
            states.append(s_new)
            x = _matmul_resid(grp, o, p["w_ret_out"], l, x, (mod, l), 2)
        else:
            j = l - n_a
            lam_init = 0.8 - 0.6 * math.exp(-0.3 * l)
            (q,) = _nm_matmul(grp, x, p["norm_mix"], l, (mod, l), 0, 1, p["w_q_b"], j,
                              [(nh_b * 2 * dh_b, BF16, "plain", 1.0)])
            o = mixer["attention"](j, lam_init, q, k_sh, v_sh)
            x = _matmul_resid(grp, o, p["w_o_b"], j, x, (mod, l), 2)
        i = l // 2
        if l % 2 == 0:
            x = _ffn_dense(grp, x, p["norm_ffn"], l, (mod, l),
                           p["w_ffn_gate"], p["w_ffn_up"], p["w_ffn_down"], i)
        else:
            x = _moe(grp, x, p["norm_ffn"], l, (mod, l), p["w_router"], p["b_router"],
                     p["w_exp_gate"], p["w_exp_up"], p["w_exp_down"], i, mixer["moe_tile"])
        if l == n_a - 1:
            wb = nh_b * 2 * dh_b
            k_sh, v_sh = _nm_matmul(grp, x, p["norm_kv"].reshape(1, d), 0, (mod_kv, 0), 0, 1,
                                    p["w_kv"].reshape(1, d, 2 * wb), 0,
                                    [(wb, F32, "plain", 1.0), (wb, F32, "plain", 1.0)])
    y = _final_norm(grp, x, p["norm_f"], (mod_f, 0))
    return y, states, k_sh, v_sh


def kernel(x_prompt, x_sample, state_ret, cache_k, cache_v, page_table, c_prompt, c_sample,
           w_ada, b_ada, norm_mix, norm_ffn, w_ret_in, w_ret_out, w_ada_kv, b_ada_kv, norm_kv, w_kv,
           w_q_b, w_o_b, lam_b, subln_b, w_ffn_gate, w_ffn_up, w_ffn_down, w_router, b_router,
           w_exp_gate, w_exp_up, w_exp_down, w_ada_f, b_ada_f, norm_f):
    p = dict(norm_mix=norm_mix, norm_ffn=norm_ffn, w_ret_in=w_ret_in, w_ret_out=w_ret_out,
             norm_kv=norm_kv, w_kv=w_kv, w_q_b=w_q_b, w_o_b=w_o_b, w_ffn_gate=w_ffn_gate,
             w_ffn_up=w_ffn_up, w_ffn_down=w_ffn_down, w_router=w_router, b_router=b_router,
             w_exp_gate=w_exp_gate, w_exp_up=w_exp_up, w_exp_down=w_exp_down, norm_f=norm_f)
    nb, t, d = x_prompt.shape
    db, dt, _ = x_sample.shape
    assert dt == 1 and nb <= MOD_ROWS and db == MOD_ROWS
    depth = w_ada.shape[0]
    n_a = w_ret_in.shape[0]
    nh_a, dk_a, dv_a = state_ret.shape[2], state_ret.shape[3], state_ret.shape[4]
    nh_b = cache_k.shape[2]
    dh_b = cache_k.shape[3] // 2
    page = cache_k.shape[1]
    past_len = page_table.shape[1] * page
    dims = (depth, n_a, nh_a, nh_b, dk_a, dv_a, dh_b)

    c_rows = jnp.concatenate([c_prompt, jnp.zeros((MOD_ROWS - nb, d), F32), c_sample], axis=0)
    mod = _ada(c_rows, w_ada, b_ada, 6)
    mod_kv = _ada(c_rows, w_ada_kv.reshape(1, d, 2 * d), b_ada_kv.reshape(1, 2 * d), 2)
    mod_f = _ada(c_rows, w_ada_f.reshape(1, d, 2 * d), b_ada_f.reshape(1, 2 * d), 2)
    mods = (mod, mod_kv, mod_f)

    gp = _Group(nb * t, t, _tile(t, 1024, 128), False, 0)
    mixer_p = dict(
        rope=_rope_tables(np.arange(t), dk_a),
        retention=lambda l, q, k, v, g: _retention_prompt(q, k, v, g, nb, t, nh_a),
        attention=lambda j, li, q, k, v: _diff_attn_prompt(q, k, v, lam_b, subln_b, j, li, nb, t, nh_b),
        moe_tile=_tile(TOP_K * nb * t, 1024, 128))
    y_p, st_p, k_p, v_p = _trunk(gp, x_prompt.reshape(nb * t, d), mods, p, dims, mixer_p)

    gs = _Group(db, db, db, True, 1)
    mixer_s = dict(
        rope=_rope_tables(np.full((db,), past_len), dk_a),
        retention=lambda l, q, k, v, g: _retention_step(q, k, v, g, state_ret[l], nh_a),
        attention=lambda j, li, q, k, v: _diff_attn_decode(
            q, k, v, cache_k, cache_v, page_table, lam_b, subln_b, j, li, nh_b),
        moe_tile=db)
    y_s, st_s, k_s, v_s = _trunk(gs, x_sample.reshape(db, d), mods, p, dims, mixer_s)

    return (y_p.reshape(nb, t, d), y_s.reshape(db, 1, d),
            jnp.stack(st_p), jnp.stack(st_s),
            k_p.reshape(nb, t, nh_b, 2 * dh_b), v_p.reshape(nb, t, nh_b, 2 * dh_b),
            k_s.reshape(db, 1, nh_b, 2 * dh_b), v_s.reshape(db, 1, nh_b, 2 * dh_b))
```
